```python
import math
import jax, jax.numpy as jnp
from jax import lax
import numpy as np

D_MODEL = 1024
BATCH = 8
SEQ = 2048
DEPTH = 4
DEC_BATCH = 32
DEC_SEQ = 4
PAST_LEN = 8192
PAGE_SIZE = 128

N_A_LAYERS = DEPTH // 2
N_B_LAYERS = DEPTH - N_A_LAYERS
A_HEADS = 8
A_DK = D_MODEL // A_HEADS
A_DV = D_MODEL // A_HEADS
A_WIDTH = A_HEADS * A_DK
GLA_CHUNK = 32
B_HEADS = 8
B_DH = D_MODEL // B_HEADS // 2
B_VDIM = 2 * B_DH
B_WIDTH = B_HEADS * 2 * B_DH
Q_BLOCK = 128
MOE_GROUPS = 4
MOE_EPG = 4
MOE_EXPERTS = MOE_GROUPS * MOE_EPG
MOE_TOP_K = 2
MOE_D_EXPERT = 256
NORM_EPS = 1e-6
SUBLN_EPS = 1e-5
NEG_BIG = -1e30

kernel_name = 'yoco_hgrn2_diffattn_hmoe_step'

F32 = jnp.float32


def rmsnorm(x, w, eps=NORM_EPS):
    xf = x.astype(F32)
    y = xf * lax.rsqrt(jnp.mean(xf * xf, axis=-1, keepdims=True) + eps)
    return (y * w.astype(F32)).astype(x.dtype)


def lower_bounds(logits):
    p = jax.nn.softmax(logits.astype(F32), axis=0)
    return jnp.cumsum(p, axis=0) - p[0:1]


def alibi_slopes():
    return 2.0 ** (-8.0 * jnp.arange(1, B_HEADS + 1, dtype=F32) / B_HEADS)


def gla_chunked(q, k, v, logf, s0):
    B, T, H, DK = q.shape
    DV = v.shape[-1]
    c = math.gcd(T, GLA_CHUNK)
    n = T // c

    def to_chunks(a):
        return a.reshape(B, n, c, H, a.shape[-1]).transpose(1, 0, 3, 2, 4)

    causal = jnp.tril(jnp.ones((c, c), dtype=bool))

    def step(S, inp):
        qb, kb, vb, fb = inp
        b = jnp.cumsum(fb, axis=2)
        inter = jnp.einsum('bhtk,bhkv->bhtv', qb * jnp.exp(b), S)
        diff = b[:, :, :, None, :] - b[:, :, None, :, :]
        decay = jnp.exp(jnp.where(causal[:, :, None], diff, -jnp.inf))
        att = jnp.einsum('bhtk,bhsk,bhtsk->bhts', qb, kb, decay)
        intra = jnp.einsum('bhts,bhsv->bhtv', att, vb)
        bl = b[:, :, -1:, :]
        S_new = jnp.exp(bl[:, :, 0, :])[..., None] * S + jnp.einsum('bhsk,bhsv->bhkv', kb * jnp.exp(bl - b), vb)
        return S_new, inter + intra

    S, o = lax.scan(step, s0, (to_chunks(q), to_chunks(k), to_chunks(v), to_chunks(logf)))
    o = o.transpose(1, 0, 3, 2, 4).reshape(B, T, H, DV)
    return o, S


def hgrn2_mixer(hn, w_in, lb, gnorm_w, w_o, s0):
    B, T, _ = hn.shape
    proj = hn @ w_in
    q, f, i, g = jnp.split(proj, 4, axis=-1)
    fg = lb + (1.0 - lb) * jax.nn.sigmoid(f.astype(F32))
    logf = jnp.log(fg)
    k = 1.0 - fg

    def heads(a):
        return a.reshape(B, T, A_HEADS, -1)

    o, s_new = gla_chunked(heads(jax.nn.silu(q.astype(F32))), heads(k), heads(i.astype(F32)),
                           heads(logf), s0.astype(F32))
    o = rmsnorm(o, gnorm_w).reshape(B, T, A_WIDTH) * jax.nn.silu(g.astype(F32))
    return o.astype(hn.dtype) @ w_o, s_new


def diff_attn_prompt(q, k, v, lam):
    B, S = q.shape[:2]
    qb_len = math.gcd(S, Q_BLOCK)
    nb = S // qb_len
    slopes = alibi_slopes()
    kf = k.astype(F32)
    vf = v.astype(F32)
    qs = (q.astype(F32) * B_DH ** -0.5).reshape(B, nb, qb_len, B_HEADS, 2, B_DH).transpose(1, 0, 2, 3, 4, 5)
    kpos = jnp.arange(S)

    def block(args):
        qi, bi = args
        qpos = bi * qb_len + jnp.arange(qb_len)
        dist = qpos[:, None] - kpos[None, :]
        s = jnp.einsum('bqhcd,bkhcd->bhcqk', qi, kf) - slopes[:, None, None, None] * dist.astype(F32)
        s = jnp.where(dist >= 0, s, -jnp.inf)
        p = jax.nn.softmax(s, axis=-1)
        a = p[:, :, 0] - lam * p[:, :, 1]
        return jnp.einsum('bhqk,bkhv->bqhv', a, vf)

    o = lax.map(block, (qs, jnp.arange(nb)))
    return o.transpose(1, 0, 2, 3, 4).reshape(B, S, B_HEADS, B_VDIM)


def diff_attn_sample(q, k_new, v_new, cache_k, cache_v, page_table, lam):
    DB, T = q.shape[:2]
    n_pages = page_table.shape[1]
    past = n_pages * PAGE_SIZE
    slopes = alibi_slopes()
    qs = q.astype(F32) * B_DH ** -0.5
    qpos = past + jnp.arange(T)

    def alibi(kpos):
        return (-slopes[:, None, None] * (qpos[:, None] - kpos[None, :]).astype(F32))[None, :, None]

    def merge(carry, s, vv):
        m, l, acc = carry
        m_new = jnp.maximum(m, s.max(axis=-1))
        pe = jnp.exp(s - m_new[..., None])
        corr = jnp.exp(m - m_new)
        l = l * corr + pe.sum(axis=-1)
        acc = acc * corr[..., None] + jnp.einsum('bhctk,bkhv->bhctv', pe, vv)
        return (m_new, l, acc)

    def page_step(carry, p):
        phys = page_table[:, p]
        kp = cache_k[phys].astype(F32).reshape(DB, PAGE_SIZE, B_HEADS, 2, B_DH)
        vp = cache_v[phys].astype(F32)
        s = jnp.einsum('bthcd,bkhcd->bhctk', qs, kp) + alibi(p * PAGE_SIZE + jnp.arange(PAGE_SIZE))
        return merge(carry, s, vp), None

    init = (jnp.full((DB, B_HEADS, 2, T), NEG_BIG, F32), jnp.zeros((DB, B_HEADS, 2, T), F32),
            jnp.zeros((DB, B_HEADS, 2, T, B_VDIM), F32))
    carry, _ = lax.scan(page_step, init, jnp.arange(n_pages))
    s = jnp.einsum('bthcd,bkhcd->bhctk', qs, k_new.astype(F32)) + alibi(qpos)
    s = jnp.where(jnp.tril(jnp.ones((T, T), dtype=bool)), s, -jnp.inf)
    m, l, acc = merge(carry, s, v_new.astype(F32))
    o = acc / l[..., None]
    out = o[:, :, 0] - lam * o[:, :, 1]
    return out.transpose(0, 2, 1, 3)


def hier_moe(h, w_group, w_sub, w_gate, w_up, w_down):
    hf = h.astype(F32)
    pg = jax.nn.softmax(hf @ w_group.astype(F32), axis=-1)
    gidx = jnp.argmax(pg, axis=-1)
    gprob = jnp.max(pg, axis=-1)
    sub_all = jnp.einsum('btd,gde->btge', hf, w_sub.astype(F32))
    sub = jnp.einsum('btg,btge->bte', jax.nn.one_hot(gidx, MOE_GROUPS, dtype=F32), sub_all)
    vals, idx = lax.top_k(sub, MOE_TOP_K)
    w2 = jax.nn.softmax(vals, axis=-1) * gprob[..., None]
    eid = gidx[..., None] * MOE_EPG + idx
    gates = jnp.einsum('btke,btk->bte', jax.nn.one_hot(eid, MOE_EXPERTS, dtype=F32), w2)
    hid = jax.nn.silu(jnp.einsum('btd,edf->btef', h, w_gate)) * jnp.einsum('btd,edf->btef', h, w_up)
    hid = hid * gates.astype(h.dtype)[..., None]
    return jnp.einsum('btef,efd->btd', hid, w_down)


def _trunk(x, hgrn_s0, cache_k, cache_v, page_table, norm_mix_w, norm_ffn_w, a_w_in, a_lb_logits, a_gnorm_w,
           a_w_o, kv_norm_w, w_kv, b_w_q, b_lambda_q1, b_lambda_k1, b_lambda_q2, b_lambda_k2, b_subln_w, b_w_o,
           moe_w_group, moe_w_sub, moe_w_gate, moe_w_up, moe_w_down, final_norm_w):
    B, T, _ = x.shape
    sample = cache_k is not None
    lbs = lower_bounds(a_lb_logits)
    new_states = []
    k_sh = None
    v_sh = None
    h = x
    for l in range(DEPTH):
        hn = rmsnorm(h, norm_mix_w[l])
        if l < N_A_LAYERS:
            s0 = hgrn_s0[l] if sample else jnp.zeros((B, A_HEADS, A_DK, A_DV), F32)
            mix, s_new = hgrn2_mixer(hn, a_w_in[l], lbs[l], a_gnorm_w[l], a_w_o[l], s0)
            new_states.append(s_new.astype(x.dtype))
        else:
            j = l - N_A_LAYERS
            lam_init = 0.8 - 0.6 * math.exp(-0.3 * l)
            lam = (jnp.exp(jnp.sum(b_lambda_q1[j].astype(F32) * b_lambda_k1[j].astype(F32)))
                   - jnp.exp(jnp.sum(b_lambda_q2[j].astype(F32) * b_lambda_k2[j].astype(F32))) + lam_init)
            q = (hn @ b_w_q[j]).reshape(B, T, B_HEADS, 2, B_DH)
            kk = k_sh.reshape(B, T, B_HEADS, 2, B_DH)
            if sample:
                o = diff_attn_sample(q, kk, v_sh, cache_k, cache_v, page_table, lam)
            else:
                o = diff_attn_prompt(q, kk, v_sh, lam)
            o = rmsnorm(o, b_subln_w[j], SUBLN_EPS) * (1.0 - lam_init)
            mix = o.astype(x.dtype).reshape(B, T, B_WIDTH) @ b_w_o[j]
        h = h + mix
        h = h + hier_moe(rmsnorm(h, norm_ffn_w[l]), moe_w_group[l], moe_w_sub[l], moe_w_gate[l],
                         moe_w_up[l], moe_w_down[l])
        if l == N_A_LAYERS - 1:
            kv = rmsnorm(h, kv_norm_w) @ w_kv
            k_sh = kv[..., :B_WIDTH].reshape(B, T, B_HEADS, 2 * B_DH)
            v_sh = kv[..., B_WIDTH:].reshape(B, T, B_HEADS, B_VDIM)
    y = rmsnorm(h, final_norm_w)
    return y, jnp.stack(new_states), k_sh, v_sh


def setup_inputs(seed: int = 0) -> dict:
    key = jax.random.key(seed)
    ks = jax.random.split(key, 32)
    n_pages = PAST_LEN // PAGE_SIZE
    n_used = DEC_BATCH * n_pages
    n_pool = n_used + n_used // 4
    out_scale = (2 * DEPTH) ** -0.5

    def nrm(k, shape, scale):
        return jax.random.normal(k, shape, F32) * scale

    return {
        'x_prompt': nrm(ks[0], (BATCH, SEQ, D_MODEL), 1.0),
        'x_sample': nrm(ks[1], (DEC_BATCH, DEC_SEQ, D_MODEL), 1.0),
        'state_hgrn': nrm(ks[2], (N_A_LAYERS, DEC_BATCH, A_HEADS, A_DK, A_DV), 0.5),
        'cache_k': nrm(ks[3], (n_pool, PAGE_SIZE, B_HEADS, 2 * B_DH), 1.0),
        'cache_v': nrm(ks[4], (n_pool, PAGE_SIZE, B_HEADS, B_VDIM), 1.0),
        'page_table': jax.random.permutation(ks[5], n_pool)[:n_used].reshape(DEC_BATCH, n_pages).astype(jnp.int32),
        'norm_mix_w': 1.0 + nrm(ks[6], (DEPTH, D_MODEL), 0.02),
        'norm_ffn_w': 1.0 + nrm(ks[7], (DEPTH, D_MODEL), 0.02),
        'a_w_in': nrm(ks[8], (N_A_LAYERS, D_MODEL, 4 * A_WIDTH), D_MODEL ** -0.5),
        'a_lb_logits': nrm(ks[9], (N_A_LAYERS, A_WIDTH), 0.5),
        'a_gnorm_w': 1.0 + nrm(ks[10], (N_A_LAYERS, A_DV), 0.02),
        'a_w_o': nrm(ks[11], (N_A_LAYERS, A_WIDTH, D_MODEL), A_WIDTH ** -0.5 * out_scale),
        'kv_norm_w': 1.0 + nrm(ks[12], (D_MODEL,), 0.02),
        'w_kv': nrm(ks[13], (D_MODEL, 2 * B_WIDTH), D_MODEL ** -0.5),
        'b_w_q': nrm(ks[14], (N_B_LAYERS, D_MODEL, B_WIDTH), D_MODEL ** -0.5),
        'b_lambda_q1': nrm(ks[15], (N_B_LAYERS, B_DH), 0.1),
        'b_lambda_k1': nrm(ks[16], (N_B_LAYERS, B_DH), 0.1),
        'b_lambda_q2': nrm(ks[17], (N_B_LAYERS, B_DH), 0.1),
        'b_lambda_k2': nrm(ks[18], (N_B_LAYERS, B_DH), 0.1),
        'b_subln_w': 1.0 + nrm(ks[19], (N_B_LAYERS, B_VDIM), 0.02),
        'b_w_o': nrm(ks[20], (N_B_LAYERS, B_WIDTH, D_MODEL), B_WIDTH ** -0.5 * out_scale),
        'moe_w_group': nrm(ks[21], (DEPTH, D_MODEL, MOE_GROUPS), D_MODEL ** -0.5),
        'moe_w_sub': nrm(ks[22], (DEPTH, MOE_GROUPS, D_MODEL, MOE_EPG), D_MODEL ** -0.5),
        'moe_w_gate': nrm(ks[23], (DEPTH, MOE_EXPERTS, D_MODEL, MOE_D_EXPERT), D_MODEL ** -0.5),
        'moe_w_up': nrm(ks[24], (DEPTH, MOE_EXPERTS, D_MODEL, MOE_D_EXPERT), D_MODEL ** -0.5),
        'moe_w_down': nrm(ks[25], (DEPTH, MOE_EXPERTS, MOE_D_EXPERT, D_MODEL), MOE_D_EXPERT ** -0.5 * out_scale),
        'final_norm_w': 1.0 + nrm(ks[26], (D_MODEL,), 0.02),
    }


def reference(x_prompt, x_sample, state_hgrn, cache_k, cache_v, page_table, norm_mix_w, norm_ffn_w, a_w_in,
              a_lb_logits, a_gnorm_w, a_w_o, kv_norm_w, w_kv, b_w_q, b_lambda_q1, b_lambda_k1, b_lambda_q2,
              b_lambda_k2, b_subln_w, b_w_o, moe_w_group, moe_w_sub, moe_w_gate, moe_w_up, moe_w_down,
              final_norm_w):
    y_prompt, state_hgrn_prompt, k_prompt, v_prompt = _trunk(
        x_prompt, None, None, None, None, norm_mix_w, norm_ffn_w, a_w_in, a_lb_logits, a_gnorm_w, a_w_o,
        kv_norm_w, w_kv, b_w_q, b_lambda_q1, b_lambda_k1, b_lambda_q2, b_lambda_k2, b_subln_w, b_w_o,
        moe_w_group, moe_w_sub, moe_w_gate, moe_w_up, moe_w_down, final_norm_w)
    y_sample, state_hgrn_sample, k_sample, v_sample = _trunk(
        x_sample, state_hgrn, cache_k, cache_v, page_table, norm_mix_w, norm_ffn_w, a_w_in, a_lb_logits,
        a_gnorm_w, a_w_o, kv_norm_w, w_kv, b_w_q, b_lambda_q1, b_lambda_k1, b_lambda_q2, b_lambda_k2,
        b_subln_w, b_w_o, moe_w_group, moe_w_sub, moe_w_gate, moe_w_up, moe_w_down, final_norm_w)
    return (y_prompt, y_sample, state_hgrn_prompt, state_hgrn_sample, k_prompt, v_prompt, k_sample, v_sample)
```

```python
import functools
import math

import jax
import jax.numpy as jnp
from jax import lax
from jax.experimental import pallas as pl
from jax.experimental.pallas import tpu as pltpu

F32 = jnp.float32
BF16 = jnp.bfloat16
HIGHEST = lax.Precision.HIGHEST

LANES = 128
SUBLANES = 8
VMEM_LIMIT_BYTES = 48 * 1024 * 1024

NORM_EPS = 1e-6
SUBLN_EPS = 1e-5
NEG_BIG = -1e30
GLA_CHUNK = 128
MOE_GROUPS = 4
MOE_EPG = 4
PAGE_SIZE = 128


def _cparams(sem):
    return pltpu.CompilerParams(dimension_semantics=sem, vmem_limit_bytes=VMEM_LIMIT_BYTES)


def _rms(x, w, eps):
    var = jnp.mean(x * x, axis=-1, keepdims=True)
    return x * lax.rsqrt(var + eps) * w


def _sigmoid(x):
    return 1.0 / (1.0 + jnp.exp(-x))


def _dot(a, b):
    return jnp.dot(a, b, preferred_element_type=F32)


def _dot_f32(a, b):
    return jnp.dot(a, b, precision=HIGHEST, preferred_element_type=F32)


def _dot_nt(a, b):
    return lax.dot_general(a, b, (((1,), (1,)), ((), ())), preferred_element_type=F32)


def _dot_tn(a, b):
    return lax.dot_general(a, b, (((0,), (0,)), ((), ())), preferred_element_type=F32)


def _row_tile(t, pref):
    tm = min(pref, t)
    assert t % tm == 0, (t, tm)
    return tm


def _norm_matmul_kernel(x_ref, nw_ref, w_ref, *out_refs, groups, nchunk):
    hn = _rms(x_ref[...], nw_ref[...], NORM_EPS)
    precise = w_ref.dtype == F32
    if not precise:
        hn = hn.astype(BF16)
    k = 0
    for (c0, width, scale, n_out) in groups:
        refs = out_refs[k:k + n_out]
        k += n_out
        for j in range(0, width, nchunk):
            wblk = w_ref[:, c0 + j:c0 + j + nchunk]
            acc = _dot_f32(hn, wblk) if precise else _dot(hn, wblk)
            if scale != 1.0:
                acc = acc * scale
            for r in refs:
                r[:, j:j + nchunk] = acc.astype(r.dtype)


def norm_matmul(x, nw, w, groups, out_dtypes, name):
    t, d = x.shape
    tm = _row_tile(t, 256)
    widths = []
    for (c0, width, scale, n_out) in groups:
        widths += [width] * n_out
    nchunk = min(512, min(widths))
    out_shape = [jax.ShapeDtypeStruct((t, wd), dt) for wd, dt in zip(widths, out_dtypes)]
    return pl.pallas_call(
        functools.partial(_norm_matmul_kernel, groups=tuple(groups), nchunk=nchunk),
        out_shape=out_shape,
        grid=(t // tm,),
        in_specs=[pl.BlockSpec((tm, d), lambda i: (i, 0)),
                  pl.BlockSpec((1, d), lambda i: (0, 0)),
                  pl.BlockSpec(w.shape, lambda i: (0, 0))],
        out_specs=[pl.BlockSpec((tm, wd), lambda i: (i, 0)) for wd in widths],
        compiler_params=_cparams(("parallel",)),
        name=name,
    )(x, nw.reshape(1, d), w)


def _matmul_res_kernel(x_ref, w_ref, r_ref, o_ref):
    if w_ref.dtype == F32:
        o_ref[...] = r_ref[...] + _dot_f32(x_ref[...], w_ref[...])
    else:
        o_ref[...] = r_ref[...] + _dot(x_ref[...], w_ref[...])


def matmul_residual(x, w, res, name):
    t, kdim = x.shape
    n = w.shape[1]
    tm = _row_tile(t, 512)
    return pl.pallas_call(
        _matmul_res_kernel,
        out_shape=jax.ShapeDtypeStruct((t, n), F32),
        grid=(t // tm,),
        in_specs=[pl.BlockSpec((tm, kdim), lambda i: (i, 0)),
                  pl.BlockSpec((kdim, n), lambda i: (0, 0)),
                  pl.BlockSpec((tm, n), lambda i: (i, 0))],
        out_specs=pl.BlockSpec((tm, n), lambda i: (i, 0)),
        compiler_params=_cparams(("parallel",)),
        name=name,
    )(x, w, res)


def _rmsnorm_kernel(x_ref, w_ref, o_ref):
    o_ref[...] = _rms(x_ref[...], w_ref[...], NORM_EPS)


def rmsnorm_rows(x, w, name):
    t, d = x.shape
    tm = _row_tile(t, 512)
    return pl.pallas_call(
        _rmsnorm_kernel,
        out_shape=jax.ShapeDtypeStruct((t, d), F32),
        grid=(t // tm,),
        in_specs=[pl.BlockSpec((tm, d), lambda i: (i, 0)), pl.BlockSpec((1, d), lambda i: (0, 0))],
        out_specs=pl.BlockSpec((tm, d), lambda i: (i, 0)),
        compiler_params=_cparams(("parallel",)),
        name=name,
    )(x, w.reshape(1, d))


def _forget_lower_bound(logits, layer):
    le = jnp.exp(logits - jnp.max(logits, axis=0, keepdims=True))
    lp = le / jnp.sum(le, axis=0, keepdims=True)
    lb = jnp.zeros((1, logits.shape[1]), F32)
    for li in range(1, layer + 1):
        lb = lb + lp[li:li + 1, :]
    return lb


def _gla_kernel(q_ref, f_ref, i_ref, g_ref, lbl_ref, gw_ref, s0_ref, o_ref, sn_ref, st_ref,
                *, chunk, layer, heads, dk):
    n = pl.program_id(1)
    c = chunk

    @pl.when(n == 0)
    def _():
        for h in range(heads):
            st_ref[h] = s0_ref[0, h].T

    lb_all = _forget_lower_bound(lbl_ref[...], layer)

    row = lax.broadcasted_iota(jnp.int32, (c, dk), 0)
    rr = lax.broadcasted_iota(jnp.int32, (c, c), 0)
    cc = lax.broadcasted_iota(jnp.int32, (c, c), 1)
    tri = (rr >= cc).astype(F32)
    ones_b = jnp.ones((dk, c), BF16)
    gw = gw_ref[...]
    nblk = c // SUBLANES
    srow = lax.broadcasted_iota(jnp.int32, (SUBLANES, dk), 0)
    lane_c = lax.broadcasted_iota(jnp.int32, (SUBLANES, c), 1)

    for h in range(heads):
        sl = slice(h * dk, (h + 1) * dk)
        lb = lb_all[:, sl]
        fg = lb + (1.0 - lb) * _sigmoid(f_ref[0, :, sl])
        logf = jnp.log(fg)
        kk = 1.0 - fg
        qf = q_ref[0, :, sl]
        qs = qf * _sigmoid(qf)
        v = i_ref[0, :, sl]
        vb = v.astype(BF16)
        b = lax.dot_general(tri, logf, (((1,), (0,)), ((), ())), precision=HIGHEST,
                            preferred_element_type=F32)
        st = st_ref[h]
        inter = _dot_nt((qs * jnp.exp(b)).astype(BF16), st.astype(BF16))

        att = jnp.zeros((c, c), F32)
        half = c // 2
        while half >= SUBLANES:
            lower = ((row // half) % 2) == 1
            mids = []
            for blk in range(c // (2 * half)):
                mid = b[blk * 2 * half + half - 1: blk * 2 * half + half, :]
                mids.append(jnp.broadcast_to(mid, (2 * half, dk)))
            bmid = mids[0] if len(mids) == 1 else jnp.concatenate(mids, axis=0)
            ex = jnp.exp(jnp.where(lower, b - bmid, bmid - b))
            qd = jnp.where(lower, qs * ex, 0.0).astype(BF16)
            kd = jnp.where(lower, 0.0, kk * ex).astype(BF16)
            a = _dot_nt(qd, kd)
            same = (rr // (2 * half)) == (cc // (2 * half))
            att = att + jnp.where(same, a, 0.0)
            half //= 2

        pieces = []
        for blk in range(nblk):
            r0 = blk * SUBLANES
            bb = b[r0:r0 + SUBLANES, :]
            qb = qs[r0:r0 + SUBLANES, :]
            kb = kk[r0:r0 + SUBLANES, :]
            for s in range(SUBLANES):
                keep = srow >= s
                dec = jnp.exp(jnp.where(keep, bb - bb[s:s + 1, :], 0.0))
                pieces.append(jnp.where(keep, qb * kb[s:s + 1, :] * dec, 0.0).astype(BF16))
        pst = jnp.concatenate(pieces, axis=0)
        red = _dot(pst, ones_b)
        dblocks = []
        for blk in range(nblk):
            acc = jnp.zeros((SUBLANES, c), F32)
            for s in range(SUBLANES):
                idx = blk * SUBLANES + s
                acc = jnp.where(lane_c == idx, red[idx * SUBLANES:(idx + 1) * SUBLANES, :], acc)
            dblocks.append(acc)
        att = att + jnp.concatenate(dblocks, axis=0)

        out = inter + _dot(att.astype(BF16), vb)
        out = _rms(out, gw, NORM_EPS)
        gf = g_ref[0, :, sl]
        o_ref[0, :, sl] = (out * (gf * _sigmoid(gf))).astype(o_ref.dtype)

        bl = b[c - 1:c, :]
        kdl = (kk * jnp.exp(bl - b)).astype(BF16)
        st_new = st * jnp.exp(bl) + _dot_tn(vb, kdl)
        st_ref[h] = st_new

    @pl.when(n == pl.num_programs(1) - 1)
    def _():
        for h in range(heads):
            sn_ref[0, h] = st_ref[h].T


def gla(proj, lb_logits, gnorm_w, s0, layer, name):
    bsz, s, w4 = proj.shape
    w = w4 // 4
    _, heads, dk, dv = s0.shape
    c = min(GLA_CHUNK, s)
    assert s % c == 0 and c % (2 * SUBLANES) == 0 and dk == dv == LANES
    nl = lb_logits.shape[0]
    kern = functools.partial(_gla_kernel, chunk=c, layer=layer, heads=heads, dk=dk)
    return pl.pallas_call(
        kern,
        out_shape=[jax.ShapeDtypeStruct((bsz, s, w), BF16), jax.ShapeDtypeStruct(s0.shape, F32)],
        grid=(bsz, s // c),
        in_specs=[pl.BlockSpec((1, c, w), lambda b, n: (b, n, 0)),
                  pl.BlockSpec((1, c, w), lambda b, n: (b, n, 1)),
                  pl.BlockSpec((1, c, w), lambda b, n: (b, n, 2)),
                  pl.BlockSpec((1, c, w), lambda b, n: (b, n, 3)),
                  pl.BlockSpec((nl, w), lambda b, n: (0, 0)),
                  pl.BlockSpec((1, dv), lambda b, n: (0, 0)),
                  pl.BlockSpec((1, heads, dk, dv), lambda b, n: (b, 0, 0, 0))],
        out_specs=[pl.BlockSpec((1, c, w), lambda b, n: (b, n, 0)),
                   pl.BlockSpec((1, heads, dk, dv), lambda b, n: (b, 0, 0, 0))],
        scratch_shapes=[pltpu.VMEM((heads, dv, dk), F32)],
        compiler_params=_cparams(("parallel", "arbitrary")),
        name=name,
    )(proj, proj, proj, proj, lb_logits, gnorm_w.reshape(1, dv), s0)


def _gla_steps_kernel(p_ref, lbl_ref, gw_ref, s0_ref, o_ref, sn_ref, *, steps, layer, heads, dk):
    w = heads * dk
    rows = p_ref.shape[1]
    lb_all = _forget_lower_bound(lbl_ref[...], layer)
    gw = gw_ref[...]
    for h in range(heads):
        qf = p_ref[0, :, h * dk:(h + 1) * dk]
        ff = p_ref[0, :, w + h * dk:w + (h + 1) * dk]
        v = p_ref[0, :, 2 * w + h * dk:2 * w + (h + 1) * dk]
        gf = p_ref[0, :, 3 * w + h * dk:3 * w + (h + 1) * dk]
        lb = lb_all[:, h * dk:(h + 1) * dk]
        fg = lb + (1.0 - lb) * _sigmoid(ff)
        qs = qf * _sigmoid(qf)
        stack = jnp.concatenate([qs, fg, 1.0 - fg, jnp.zeros((dk - 3 * rows, dk), F32)], axis=0)
        cols = stack.T
        st = s0_ref[0, h]
        outs = []
        for t in range(steps):
            st = st * cols[:, rows + t:rows + t + 1] + cols[:, 2 * rows + t:2 * rows + t + 1] * v[t:t + 1, :]
            outs.append(jnp.sum(st * cols[:, t:t + 1], axis=0, keepdims=True))
        sn_ref[0, h] = st
        out = jnp.concatenate(outs + [jnp.zeros((rows - steps, dk), F32)], axis=0)
        out = _rms(out, gw, NORM_EPS)
        o_ref[0, :, h * dk:(h + 1) * dk] = out * (gf * _sigmoid(gf))


def gla_steps(proj, lb_logits, gnorm_w, s0, layer, steps, name):
    bsz, rows, w4 = proj.shape
    w = w4 // 4
    _, heads, dk, dv = s0.shape
    assert rows % SUBLANES == 0 and 3 * rows <= dk and dk == dv == LANES
    nl = lb_logits.shape[0]
    return pl.pallas_call(
        functools.partial(_gla_steps_kernel, steps=steps, layer=layer, heads=heads, dk=dk),
        out_shape=[jax.ShapeDtypeStruct((bsz, rows, w), F32), jax.ShapeDtypeStruct(s0.shape, F32)],
        grid=(bsz,),
        in_specs=[pl.BlockSpec((1, rows, w4), lambda b: (b, 0, 0)),
                  pl.BlockSpec((nl, w), lambda b: (0, 0)),
                  pl.BlockSpec((1, dv), lambda b: (0, 0)),
                  pl.BlockSpec((1, heads, dk, dv), lambda b: (b, 0, 0, 0))],
        out_specs=[pl.BlockSpec((1, rows, w), lambda b: (b, 0, 0)),
                   pl.BlockSpec((1, heads, dk, dv), lambda b: (b, 0, 0, 0))],
        compiler_params=_cparams(("parallel",)),
        name=name,
    )(proj, lb_logits, gnorm_w.reshape(1, dv), s0)


def _router_logits(hn, wr_ref):
    return lax.dot_general(hn, wr_ref[...], (((1,), (0,)), ((), ())), precision=HIGHEST,
                           preferred_element_type=F32)


def _router_kernel(h_ref, nw_ref, wr_ref, gid_ref):
    hn = _rms(h_ref[...], nw_ref[...], NORM_EPS)
    logits = _router_logits(hn, wr_ref)
    lane = lax.broadcasted_iota(jnp.int32, logits.shape, 1)
    isg = lane < MOE_GROUPS
    pgl = jnp.where(isg, logits, -jnp.inf)
    m = jnp.max(pgl, axis=-1, keepdims=True)
    gidx = jnp.min(jnp.where(isg & (pgl == m), lane, LANES), axis=-1, keepdims=True)
    gid_ref[...] = jnp.broadcast_to(gidx, gid_ref.shape)


def moe_router(h, nw, wr, name):
    t, d = h.shape
    tm = _row_tile(t, 512)
    return pl.pallas_call(
        _router_kernel,
        out_shape=jax.ShapeDtypeStruct((t, LANES), jnp.int32),
        grid=(t // tm,),
        in_specs=[pl.BlockSpec((tm, d), lambda i: (i, 0)),
                  pl.BlockSpec((1, d), lambda i: (0, 0)),
                  pl.BlockSpec(wr.shape, lambda i: (0, 0))],
        out_specs=pl.BlockSpec((tm, LANES), lambda i: (i, 0)),
        compiler_params=_cparams(("parallel",)),
        name=name,
    )(h, nw.reshape(1, d), wr)


def _moe_ffn_kernel(tg_ref, nv_ref, ridx_ref, h_hbm, nw_ref, wr_ref, wg_ref, wu_ref, wd_ref, out_hbm,
                    xbuf, obuf, gsem, ssem, *, tm):
    t = pl.program_id(0)
    nv = nv_ref[t]
    g = tg_ref[t]
    base = t * tm

    def row_in(r):
        return pltpu.make_async_copy(h_hbm.at[pl.ds(ridx_ref[base + r], 1)], xbuf.at[pl.ds(r, 1)], gsem)

    def row_out(r):
        return pltpu.make_async_copy(obuf.at[pl.ds(r, 1)], out_hbm.at[pl.ds(ridx_ref[base + r], 1)], ssem)

    @pl.when(nv > 0)
    def _():
        def issue_in(r, carry):
            row_in(r).start()
            return carry

        def wait_in(r, carry):
            row_in(r).wait()
            return carry

        lax.fori_loop(0, tm, issue_in, 0)
        lax.fori_loop(0, tm, wait_in, 0)

        x = xbuf[...]
        hn = _rms(x, nw_ref[...], NORM_EPS)
        logits = _router_logits(hn, wr_ref)
        lane = lax.broadcasted_iota(jnp.int32, logits.shape, 1)
        isg = lane < MOE_GROUPS
        m = jnp.max(jnp.where(isg, logits, -jnp.inf), axis=-1, keepdims=True)
        e = jnp.where(isg, jnp.exp(jnp.where(isg, logits - m, 0.0)), 0.0)
        gprob = jnp.sum(jnp.where(lane == g, e, 0.0), axis=-1, keepdims=True) / jnp.sum(e, axis=-1, keepdims=True)
        lo = MOE_GROUPS + MOE_EPG * g
        inb = (lane >= lo) & (lane < lo + MOE_EPG)
        sv = jnp.where(inb, logits, -jnp.inf)
        m1 = jnp.max(sv, axis=-1, keepdims=True)
        i1 = jnp.min(jnp.where(inb & (sv == m1), lane, LANES), axis=-1, keepdims=True)
        inb2 = inb & (lane != i1)
        sv2 = jnp.where(inb2, logits, -jnp.inf)
        m2 = jnp.max(sv2, axis=-1, keepdims=True)
        i2 = jnp.min(jnp.where(inb2 & (sv2 == m2), lane, LANES), axis=-1, keepdims=True)
        e2 = jnp.exp(m2 - m1)
        w1 = gprob / (1.0 + e2)
        w2 = gprob * e2 / (1.0 + e2)
        gate = jnp.where(lane == i1, w1, jnp.where(lane == i2, w2, 0.0))

        precise = wg_ref.dtype == F32
        hb = hn if precise else hn.astype(BF16)
        dot = _dot_f32 if precise else _dot
        y = jnp.zeros(x.shape, F32)
        for ex in range(MOE_EPG):
            ge = jnp.sum(jnp.where(lane == lo + ex, gate, 0.0), axis=-1, keepdims=True)
            a = dot(hb, wg_ref[ex])
            u = dot(hb, wu_ref[ex])
            hid = a * _sigmoid(a) * u * ge
            y = y + dot(hid if precise else hid.astype(BF16), wd_ref[ex])
        obuf[...] = x + y

        def issue_out(r, carry):
            row_out(r).start()
            return carry

        def wait_out(r, carry):
            row_out(r).wait()
            return carry

        lax.fori_loop(0, nv, issue_out, 0)
        lax.fori_loop(0, nv, wait_out, 0)


def _moe_plan(gidx, t, tm):
    ng = MOE_GROUPS
    n_tiles = t // tm + ng
    counts = jnp.sum((gidx[:, None] == jnp.arange(ng)[None, :]).astype(jnp.int32), axis=0)
    tiles_g = (counts + tm - 1) // tm
    tiles_end = jnp.cumsum(tiles_g)
    first_tile = tiles_end - tiles_g
    start_g = jnp.cumsum(counts) - counts
    used = tiles_end[-1]
    perm = jnp.argsort(gidx, stable=True).astype(jnp.int32)
    tile_ids = jnp.arange(n_tiles, dtype=jnp.int32)
    tg = jnp.sum((tile_ids[:, None] >= tiles_end[None, :]).astype(jnp.int32), axis=1)
    last_g = jnp.sum((used - 1 >= tiles_end).astype(jnp.int32))
    tg = jnp.where(tile_ids < used, jnp.minimum(tg, ng - 1), last_g).astype(jnp.int32)
    j = tile_ids - first_tile[tg]
    nv = jnp.clip(counts[tg] - j * tm, 0, tm)
    nv = jnp.where(tile_ids < used, nv, 0).astype(jnp.int32)
    src0 = start_g[tg] + j * tm
    src = src0[:, None] + jnp.arange(tm, dtype=jnp.int32)[None, :]
    ok = jnp.arange(tm, dtype=jnp.int32)[None, :] < nv[:, None]
    src = jnp.where(ok, src, src0[:, None])
    ridx = perm[jnp.clip(src, 0, t - 1)].reshape(-1).astype(jnp.int32)
    return tg, nv, ridx


def moe_layer(h, nw, wr, wg, wu, wd, name):
    t, d = h.shape
    tm = _row_tile(t, 256)
    gid = moe_router(h, nw, wr, name + "_router")
    tg, nv, ridx = _moe_plan(gid[:, 0], t, tm)
    n_tiles = t // tm + MOE_GROUPS
    fdim = wg.shape[2]
    grid_spec = pltpu.PrefetchScalarGridSpec(
        num_scalar_prefetch=3,
        grid=(n_tiles,),
        in_specs=[pl.BlockSpec(memory_space=pl.ANY),
                  pl.BlockSpec((1, d), lambda i, tg, nv, ri: (0, 0)),
                  pl.BlockSpec(wr.shape, lambda i, tg, nv, ri: (0, 0)),
                  pl.BlockSpec((MOE_EPG, d, fdim), lambda i, tg, nv, ri: (tg[i], 0, 0)),
                  pl.BlockSpec((MOE_EPG, d, fdim), lambda i, tg, nv, ri: (tg[i], 0, 0)),
                  pl.BlockSpec((MOE_EPG, fdim, d), lambda i, tg, nv, ri: (tg[i], 0, 0))],
        out_specs=pl.BlockSpec(memory_space=pl.ANY),
        scratch_shapes=[pltpu.VMEM((tm, d), F32), pltpu.VMEM((tm, d), F32),
                        pltpu.SemaphoreType.DMA, pltpu.SemaphoreType.DMA],
    )
    return pl.pallas_call(
        functools.partial(_moe_ffn_kernel, tm=tm),
        out_shape=jax.ShapeDtypeStruct((t, d), F32),
        grid_spec=grid_spec,
        compiler_params=_cparams(("arbitrary",)),
        name=name + "_ffn",
    )(tg, nv, ridx, h, nw.reshape(1, d), wr, wg, wu, wd)


def _lambda_value(lq1, lk1, lq2, lk2, lam_init):
    s1 = jnp.sum(lq1[...] * lk1[...], axis=-1, keepdims=True)
    s2 = jnp.sum(lq2[...] * lk2[...], axis=-1, keepdims=True)
    return jnp.exp(s1) - jnp.exp(s2) + lam_init


def _attn_prompt_kernel(slopes_ref, q_ref, k_ref, v_ref, lq1, lk1, lq2, lk2, sw_ref, o_ref,
                        m_sc, l_sc, acc_sc, *, bq, bk, lam_init):
    h = pl.program_id(1)
    qi = pl.program_id(2)
    ki = pl.program_id(3)
    nk = pl.num_programs(3)
    dh2 = q_ref.shape[-1]

    @pl.when(ki == 0)
    def _():
        m_sc[...] = jnp.full(m_sc.shape, NEG_BIG, F32)
        l_sc[...] = jnp.zeros(l_sc.shape, F32)
        acc_sc[...] = jnp.zeros(acc_sc.shape, F32)

    last_k = (qi * bq + bq - 1) // bk

    @pl.when(ki <= last_k)
    def _():
        slope = slopes_ref[h]
        q = q_ref[0]
        lane = lax.broadcasted_iota(jnp.int32, q.shape, 1)
        zero = jnp.zeros_like(q)
        qq = jnp.concatenate([jnp.where(lane < dh2 // 2, q, zero), jnp.where(lane >= dh2 // 2, q, zero)], axis=0)
        s = _dot_nt(qq, k_ref[0])
        r = lax.broadcasted_iota(jnp.int32, s.shape, 0)
        c = lax.broadcasted_iota(jnp.int32, s.shape, 1)
        dist = (qi * bq - ki * bk) + jnp.where(r >= bq, r - bq, r) - c
        s = jnp.where(dist >= 0, s - slope * dist.astype(F32), NEG_BIG)
        m_prev = m_sc[...]
        m_new = jnp.maximum(m_prev, jnp.max(s, axis=-1, keepdims=True))
        p = jnp.exp(s - m_new)
        corr = jnp.exp(m_prev - m_new)
        l_sc[...] = l_sc[...] * corr + jnp.sum(p, axis=-1, keepdims=True)
        acc_sc[...] = acc_sc[...] * corr + _dot(p.astype(BF16), v_ref[0])
        m_sc[...] = m_new

    @pl.when(ki == nk - 1)
    def _():
        o = acc_sc[...] / l_sc[...]
        lam = _lambda_value(lq1, lk1, lq2, lk2, lam_init)
        out = o[:bq] - lam * o[bq:]
        out = _rms(out, sw_ref[...], SUBLN_EPS) * (1.0 - lam_init)
        o_ref[0] = out.astype(o_ref.dtype)


def attn_prompt(q, k, v, slopes, lams, subln_w, lam_init, name):
    bsz, s, w = q.shape
    heads = slopes.shape[0]
    hd = w // heads
    bq = min(256, s)
    bk = min(512, s)
    nq, nk = s // bq, s // bk

    def kv_map(b, h, qi, ki):
        return (b, jnp.minimum(ki, (qi * bq + bq - 1) // bk), h)

    lam_specs = [pl.BlockSpec(lams[0].shape, lambda b, h, qi, ki: (0, 0)) for _ in range(4)]
    return pl.pallas_call(
        functools.partial(_attn_prompt_kernel, bq=bq, bk=bk, lam_init=lam_init),
        out_shape=jax.ShapeDtypeStruct((bsz, s, w), BF16),
        grid=(bsz, heads, nq, nk),
        in_specs=[pl.BlockSpec(memory_space=pltpu.SMEM),
                  pl.BlockSpec((1, bq, hd), lambda b, h, qi, ki: (b, qi, h)),
                  pl.BlockSpec((1, bk, hd), kv_map),
                  pl.BlockSpec((1, bk, hd), kv_map)] + lam_specs
                 + [pl.BlockSpec((1, hd), lambda b, h, qi, ki: (0, 0))],
        out_specs=pl.BlockSpec((1, bq, hd), lambda b, h, qi, ki: (b, qi, h)),
        scratch_shapes=[pltpu.VMEM((2 * bq, 1), F32), pltpu.VMEM((2 * bq, 1), F32),
                        pltpu.VMEM((2 * bq, hd), F32)],
        compiler_params=_cparams(("parallel", "parallel", "parallel", "arbitrary")),
        name=name,
    )(slopes, q, k, v, *lams, subln_w.reshape(1, hd))


def _attn_sample_kernel(pt_ref, qb_ref, ck_ref, cv_ref, kn_ref, vn_ref, lq1, lk1, lq2, lk2, sw_ref, o_ref,
                        m_sc, l_sc, acc_sc, *, heads, steps, n_pages, lam_init):
    p = pl.program_id(1)
    hd = sw_ref.shape[-1]
    nrow = 2 * heads * steps
    past = n_pages * PAGE_SIZE

    @pl.when(p == 0)
    def _():
        m_sc[...] = jnp.full(m_sc.shape, NEG_BIG, F32)
        l_sc[...] = jnp.zeros(l_sc.shape, F32)
        acc_sc[...] = jnp.zeros(acc_sc.shape, F32)

    r = lax.broadcasted_iota(jnp.int32, (nrow, PAGE_SIZE), 0)
    col = lax.broadcasted_iota(jnp.int32, (nrow, PAGE_SIZE), 1)
    rh = (r % (heads * steps)) // steps
    rt = r % steps
    slope = jnp.exp2(-8.0 * (rh + 1).astype(F32) / heads)

    def update(kmat, vmat, dist, ok):
        s = _dot_nt(qb_ref[0], kmat)
        s = s - slope * dist.astype(F32)
        if ok is not None:
            s = jnp.where(ok, s, NEG_BIG)
        m_prev = m_sc[...]
        m_new = jnp.maximum(m_prev, jnp.max(s, axis=-1, keepdims=True))
        pe = jnp.exp(s - m_new)
        corr = jnp.exp(m_prev - m_new)
        l_sc[...] = l_sc[...] * corr + jnp.sum(pe, axis=-1, keepdims=True)
        acc_sc[...] = acc_sc[...] * corr + _dot(pe.astype(BF16), vmat)
        m_sc[...] = m_new

    @pl.when(p < n_pages)
    def _():
        dist = past + rt - (p * PAGE_SIZE + col)
        update(ck_ref[0].astype(BF16), cv_ref[0].astype(BF16), dist, None)

    @pl.when(p == n_pages)
    def _():
        dist = rt - col
        update(kn_ref[0], vn_ref[0], dist, (dist >= 0) & (col < steps))
        acc = acc_sc[...]
        rr = lax.broadcasted_iota(jnp.int32, (nrow, hd), 0)
        rrh = (rr % (heads * steps)) // steps
        od = jnp.zeros((nrow, hd), F32)
        for hh in range(heads):
            od = od + jnp.where(rrh == hh, acc[:, hh * hd:(hh + 1) * hd], 0.0)
        od = od / l_sc[...]
        lam = _lambda_value(lq1, lk1, lq2, lk2, lam_init)
        half = heads * steps
        out = od[:half] - lam * od[half:]
        out = _rms(out, sw_ref[...], SUBLN_EPS) * (1.0 - lam_init)
        o_ref[0] = out.astype(o_ref.dtype)


def attn_sample(qbig, cache_k, cache_v, knew, vnew, page_table, lams, subln_w, lam_init, heads, steps, name):
    db, nrow, w = qbig.shape
    hd = w // heads
    n_pages = page_table.shape[1]
    pt = page_table.reshape(-1).astype(jnp.int32)

    def page_map(b, p, pt):
        return (pt[b * n_pages + jnp.minimum(p, n_pages - 1)], 0, 0)

    lam_specs = [pl.BlockSpec(lams[0].shape, lambda b, p, pt: (0, 0)) for _ in range(4)]
    grid_spec = pltpu.PrefetchScalarGridSpec(
        num_scalar_prefetch=1,
        grid=(db, n_pages + 1),
        in_specs=[pl.BlockSpec((1, nrow, w), lambda b, p, pt: (b, 0, 0)),
                  pl.BlockSpec((1, PAGE_SIZE, w), page_map),
                  pl.BlockSpec((1, PAGE_SIZE, w), page_map),
                  pl.BlockSpec((1, PAGE_SIZE, w), lambda b, p, pt: (b, 0, 0)),
                  pl.BlockSpec((1, PAGE_SIZE, w), lambda b, p, pt: (b, 0, 0))] + lam_specs
                 + [pl.BlockSpec((1, hd), lambda b, p, pt: (0, 0))],
        out_specs=pl.BlockSpec((1, heads * steps, hd), lambda b, p, pt: (b, 0, 0)),
        scratch_shapes=[pltpu.VMEM((nrow, 1), F32), pltpu.VMEM((nrow, 1), F32), pltpu.VMEM((nrow, w), F32)],
    )
    return pl.pallas_call(
        functools.partial(_attn_sample_kernel, heads=heads, steps=steps, n_pages=n_pages, lam_init=lam_init),
        out_shape=jax.ShapeDtypeStruct((db, heads * steps, hd), F32),
        grid_spec=grid_spec,
        compiler_params=_cparams(("parallel", "arbitrary")),
        name=name,
    )(pt, qbig, cache_k, cache_v, knew, vnew, *lams, subln_w.reshape(1, hd))


def _router_weights(w_group, w_sub):
    d = w_group.shape[0]
    ws = jnp.transpose(w_sub, (1, 0, 2)).reshape(d, MOE_GROUPS * MOE_EPG)
    pad = jnp.zeros((d, LANES - MOE_GROUPS - MOE_GROUPS * MOE_EPG), F32)
    return jnp.concatenate([w_group.astype(F32), ws.astype(F32), pad], axis=1)


def _block_diag_queries(q, heads, steps):
    db = q.shape[0]
    w = q.shape[-1]
    dh = w // heads // 2
    q5 = q.reshape(db, steps, heads, 2, dh)
    eye_h = jnp.eye(heads, dtype=q.dtype)
    eye_c = jnp.eye(2, dtype=q.dtype)
    big = jnp.einsum('bthcd,hg,ce->bchtged', q5, eye_h, eye_c)
    return big.reshape(db, 2 * heads * steps, w)


def _trunk(x, s0_all, paged, wts, tag):
    bsz, s, d = x.shape
    t = bsz * s
    depth = wts['norm_mix_w'].shape[0]
    n_a = wts['a_w_in'].shape[0]
    heads_a = s0_all.shape[2]
    heads_b = wts['slopes'].shape[0]
    hd = d // heads_b
    h = x.reshape(t, d)
    states = []
    k_f = v_f = k_b = v_b = None
    precise = paged is not None
    mm = wts['f32'] if precise else wts['bf16']
    for l in range(depth):
        if l < n_a:
            (proj,) = norm_matmul(h, wts['norm_mix_w'][l], mm['a_w_in'][l], [(0, 4 * d, 1.0, 1)], [F32],
                                  f"{tag}_l{l}_inproj")
            proj = proj.reshape(bsz, s, 4 * d)
            if s >= GLA_CHUNK:
                o, s_new = gla(proj, wts['a_lb_logits'], wts['a_gnorm_w'][l], s0_all[l], l, f"{tag}_l{l}_gla")
            else:
                rows = -(-s // SUBLANES) * SUBLANES
                proj = jnp.pad(proj, ((0, 0), (0, rows - s), (0, 0)))
                o, s_new = gla_steps(proj, wts['a_lb_logits'], wts['a_gnorm_w'][l], s0_all[l], l, s,
                                     f"{tag}_l{l}_gla")
                o = o[:, :s]
            states.append(s_new)
            h = matmul_residual(o.reshape(t, d), mm['a_w_o'][l], h, f"{tag}_l{l}_oproj")
        else:
            j = l - n_a
            lam_init = 0.8 - 0.6 * math.exp(-0.3 * l)
            lams = [wts[nm][j].reshape(1, -1) for nm in ('b_lambda_q1', 'b_lambda_k1', 'b_lambda_q2', 'b_lambda_k2')]
            (q,) = norm_matmul(h, wts['norm_mix_w'][l], mm['b_w_q'][j], [(0, d, (hd // 2) ** -0.5, 1)],
                               [F32 if precise else BF16], f"{tag}_l{l}_qproj")
            if paged is None:
                o = attn_prompt(q.reshape(bsz, s, d), k_b.reshape(bsz, s, d), v_b.reshape(bsz, s, d),
                                wts['slopes'], lams, wts['b_subln_w'][j], lam_init, f"{tag}_l{l}_attn")
                o = o.reshape(t, d)
            else:
                cache_k, cache_v, page_table = paged
                qbig = _block_diag_queries(q.reshape(bsz, s, d), heads_b, s).astype(BF16)
                padn = ((0, 0), (0, PAGE_SIZE - s), (0, 0))
                o = attn_sample(qbig, cache_k, cache_v, jnp.pad(k_b.reshape(bsz, s, d), padn),
                                jnp.pad(v_b.reshape(bsz, s, d), padn), page_table, lams, wts['b_subln_w'][j],
                                lam_init, heads_b, s, f"{tag}_l{l}_attn")
                o = o.reshape(bsz, heads_b, s, hd).transpose(0, 2, 1, 3).reshape(t, d)
            h = matmul_residual(o, mm['b_w_o'][j], h, f"{tag}_l{l}_oproj")
        h = moe_layer(h, wts['norm_ffn_w'][l], wts['router_w'][l], mm['moe_w_gate'][l], mm['moe_w_up'][l],
                      mm['moe_w_down'][l], f"{tag}_l{l}_moe")
        if l == n_a - 1:
            k_f, k_b, v_f, v_b = norm_matmul(h, wts['kv_norm_w'], mm['w_kv'], [(0, d, 1.0, 2), (d, d, 1.0, 2)],
                                            [F32, BF16, F32, BF16], f"{tag}_kvproj")
    y = rmsnorm_rows(h, wts['final_norm_w'], f"{tag}_final_norm")
    return (y.reshape(bsz, s, d), jnp.stack(states),
            k_f.reshape(bsz, s, heads_b, hd), v_f.reshape(bsz, s, heads_b, hd))


def kernel(x_prompt, x_sample, state_hgrn, cache_k, cache_v, page_table, norm_mix_w, norm_ffn_w, a_w_in,
           a_lb_logits, a_gnorm_w, a_w_o, kv_norm_w, w_kv, b_w_q, b_lambda_q1, b_lambda_k1, b_lambda_q2,
           b_lambda_k2, b_subln_w, b_w_o, moe_w_group, moe_w_sub, moe_w_gate, moe_w_up, moe_w_down,
           final_norm_w):
    depth = norm_mix_w.shape[0]
    heads_b = cache_k.shape[2]
    mats = dict(a_w_in=a_w_in, a_w_o=a_w_o, w_kv=w_kv, b_w_q=b_w_q, b_w_o=b_w_o,
                moe_w_gate=moe_w_gate, moe_w_up=moe_w_up, moe_w_down=moe_w_down)
    wts = dict(
        norm_mix_w=norm_mix_w, norm_ffn_w=norm_ffn_w, a_lb_logits=a_lb_logits, a_w_in=a_w_in,
        a_gnorm_w=a_gnorm_w, kv_norm_w=kv_norm_w, b_lambda_q1=b_lambda_q1, b_lambda_k1=b_lambda_k1,
        b_lambda_q2=b_lambda_q2, b_lambda_k2=b_lambda_k2, b_subln_w=b_subln_w,
        router_w=jnp.stack([_router_weights(moe_w_group[l], moe_w_sub[l]) for l in range(depth)]),
        final_norm_w=final_norm_w,
        slopes=2.0 ** (-8.0 * jnp.arange(1, heads_b + 1, dtype=F32) / heads_b),
        f32=mats, bf16={k: v.astype(BF16) for k, v in mats.items()},
    )
    n_a = a_w_in.shape[0]
    bsz = x_prompt.shape[0]
    s0_prompt = jnp.zeros((n_a, bsz) + state_hgrn.shape[2:], F32)
    y_p, st_p, k_p, v_p = _trunk(x_prompt, s0_prompt, None, wts, "p")
    pool, page, hb, hd = cache_k.shape
    paged = (cache_k.reshape(pool, page, hb * hd), cache_v.reshape(pool, page, hb * hd), page_table)
    y_s, st_s, k_s, v_s = _trunk(x_sample, state_hgrn, paged, wts, "s")
    return (y_p, y_s, st_p, st_s, k_p, v_p, k_s, v_s)
```

```python
import functools
import math

import jax
import jax.numpy as jnp
from jax import lax
from jax.experimental import pallas as pl
from jax.experimental.pallas import tpu as pltpu

F32 = jnp.float32
BF16 = jnp.bfloat16
HIGHEST = lax.Precision.HIGHEST

LANES = 128
SUBLANES = 8
VMEM_LIMIT_BYTES = 48 * 1024 * 1024

NORM_EPS = 1e-6
SUBLN_EPS = 1e-5
NEG_BIG = -1e30
GLA_CHUNK = 128
MOE_GROUPS = 4
MOE_EPG = 4
PAGE_SIZE = 128


def _cparams(sem):
    return pltpu.CompilerParams(dimension_semantics=sem, vmem_limit_bytes=VMEM_LIMIT_BYTES)


def _rms(x, w, eps):
    var = jnp.mean(x * x, axis=-1, keepdims=True)
    return x * lax.rsqrt(var + eps) * w


def _sigmoid(x):
    return 1.0 / (1.0 + jnp.exp(-x))


def _dot(a, b):
    return jnp.dot(a, b, preferred_element_type=F32)


def _dot_f32(a, b):
    return jnp.dot(a, b, precision=HIGHEST, preferred_element_type=F32)


def _dot_nt(a, b):
    return lax.dot_general(a, b, (((1,), (1,)), ((), ())), preferred_element_type=F32)


def _dot_tn(a, b):
    return lax.dot_general(a, b, (((0,), (0,)), ((), ())), preferred_element_type=F32)


def _row_tile(t, pref):
    tm = min(pref, t)
    assert t % tm == 0, (t, tm)
    return tm


def _norm_matmul_kernel(x_ref, nw_ref, w_ref, *out_refs, groups, nchunk):
    hn = _rms(x_ref[...], nw_ref[...], NORM_EPS)
    precise = w_ref.dtype == F32
    if not precise:
        hn = hn.astype(BF16)
    k = 0
    for (c0, width, scale, n_out) in groups:
        refs = out_refs[k:k + n_out]
        k += n_out
        for j in range(0, width, nchunk):
            wblk = w_ref[:, c0 + j:c0 + j + nchunk]
            acc = _dot_f32(hn, wblk) if precise else _dot(hn, wblk)
            if scale != 1.0:
                acc = acc * scale
            for r in refs:
                r[:, j:j + nchunk] = acc.astype(r.dtype)


def norm_matmul(x, nw, w, groups, out_dtypes, name):
    t, d = x.shape
    tm = _row_tile(t, 256)
    widths = []
    for (c0, width, scale, n_out) in groups:
        widths += [width] * n_out
    nchunk = min(512, min(widths))
    out_shape = [jax.ShapeDtypeStruct((t, wd), dt) for wd, dt in zip(widths, out_dtypes)]
    return pl.pallas_call(
        functools.partial(_norm_matmul_kernel, groups=tuple(groups), nchunk=nchunk),
        out_shape=out_shape,
        grid=(t // tm,),
        in_specs=[pl.BlockSpec((tm, d), lambda i: (i, 0)),
                  pl.BlockSpec((1, d), lambda i: (0, 0)),
                  pl.BlockSpec(w.shape, lambda i: (0, 0))],
        out_specs=[pl.BlockSpec((tm, wd), lambda i: (i, 0)) for wd in widths],
        compiler_params=_cparams(("parallel",)),
        name=name,
    )(x, nw.reshape(1, d), w)


def _matmul_res_kernel(x_ref, w_ref, r_ref, o_ref):
    if w_ref.dtype == F32:
        o_ref[...] = r_ref[...] + _dot_f32(x_ref[...], w_ref[...])
    else:
        o_ref[...] = r_ref[...] + _dot(x_ref[...], w_ref[...])


def matmul_residual(x, w, res, name):
    t, kdim = x.shape
    n = w.shape[1]
    tm = _row_tile(t, 512)
    return pl.pallas_call(
        _matmul_res_kernel,
        out_shape=jax.ShapeDtypeStruct((t, n), F32),
        grid=(t // tm,),
        in_specs=[pl.BlockSpec((tm, kdim), lambda i: (i, 0)),
                  pl.BlockSpec((kdim, n), lambda i: (0, 0)),
                  pl.BlockSpec((tm, n), lambda i: (i, 0))],
        out_specs=pl.BlockSpec((tm, n), lambda i: (i, 0)),
        compiler_params=_cparams(("parallel",)),
        name=name,
    )(x, w, res)


def _rmsnorm_kernel(x_ref, w_ref, o_ref):
    o_ref[...] = _rms(x_ref[...], w_ref[...], NORM_EPS)


def rmsnorm_rows(x, w, name):
    t, d = x.shape
    tm = _row_tile(t, 512)
    return pl.pallas_call(
        _rmsnorm_kernel,
        out_shape=jax.ShapeDtypeStruct((t, d), F32),
        grid=(t // tm,),
        in_specs=[pl.BlockSpec((tm, d), lambda i: (i, 0)), pl.BlockSpec((1, d), lambda i: (0, 0))],
        out_specs=pl.BlockSpec((tm, d), lambda i: (i, 0)),
        compiler_params=_cparams(("parallel",)),
        name=name,
    )(x, w.reshape(1, d))


def _forget_lower_bound(logits, layer):
    le = jnp.exp(logits - jnp.max(logits, axis=0, keepdims=True))
    lp = le / jnp.sum(le, axis=0, keepdims=True)
    lb = jnp.zeros((1, logits.shape[1]), F32)
    for li in range(1, layer + 1):
        lb = lb + lp[li:li + 1, :]
    return lb


def _gla_kernel(q_ref, f_ref, i_ref, g_ref, lbl_ref, gw_ref, s0_ref, o_ref, sn_ref, st_ref,
                *, chunk, layer, heads, dk):
    n = pl.program_id(1)
    c = chunk

    @pl.when(n == 0)
    def _():
        for h in range(heads):
            st_ref[h] = s0_ref[0, h].T

    lb_all = _forget_lower_bound(lbl_ref[...], layer)

    row = lax.broadcasted_iota(jnp.int32, (c, dk), 0)
    rr = lax.broadcasted_iota(jnp.int32, (c, c), 0)
    cc = lax.broadcasted_iota(jnp.int32, (c, c), 1)
    tri = (rr >= cc).astype(F32)
    ones_b = jnp.ones((dk, c), BF16)
    gw = gw_ref[...]
    nblk = c // SUBLANES
    srow = lax.broadcasted_iota(jnp.int32, (SUBLANES, dk), 0)
    lane_c = lax.broadcasted_iota(jnp.int32, (SUBLANES, c), 1)

    for h in range(heads):
        sl = slice(h * dk, (h + 1) * dk)
        lb = lb_all[:, sl]
        fg = lb + (1.0 - lb) * _sigmoid(f_ref[0, :, sl])
        logf = jnp.log(fg)
        kk = 1.0 - fg
        qf = q_ref[0, :, sl]
        qs = qf * _sigmoid(qf)
        v = i_ref[0, :, sl]
        vb = v.astype(BF16)
        b = lax.dot_general(tri, logf, (((1,), (0,)), ((), ())), precision=HIGHEST,
                            preferred_element_type=F32)
        st = st_ref[h]
        inter = _dot_nt((qs * jnp.exp(b)).astype(BF16), st.astype(BF16))

        att = jnp.zeros((c, c), F32)
        half = c // 2
        while half >= SUBLANES:
            lower = ((row // half) % 2) == 1
            mids = []
            for blk in range(c // (2 * half)):
                mid = b[blk * 2 * half + half - 1: blk * 2 * half + half, :]
                mids.append(jnp.broadcast_to(mid, (2 * half, dk)))
            bmid = mids[0] if len(mids) == 1 else jnp.concatenate(mids, axis=0)
            ex = jnp.exp(jnp.where(lower, b - bmid, bmid - b))
            qd = jnp.where(lower, qs * ex, 0.0).astype(BF16)
            kd = jnp.where(lower, 0.0, kk * ex).astype(BF16)
            a = _dot_nt(qd, kd)
            same = (rr // (2 * half)) == (cc // (2 * half))
            att = att + jnp.where(same, a, 0.0)
            half //= 2

        pieces = []
        for blk in range(nblk):
            r0 = blk * SUBLANES
            bb = b[r0:r0 + SUBLANES, :]
            qb = qs[r0:r0 + SUBLANES, :]
            kb = kk[r0:r0 + SUBLANES, :]
            for s in range(SUBLANES):
                keep = srow >= s
                dec = jnp.exp(jnp.where(keep, bb - bb[s:s + 1, :], 0.0))
                pieces.append(jnp.where(keep, qb * kb[s:s + 1, :] * dec, 0.0).astype(BF16))
        pst = jnp.concatenate(pieces, axis=0)
        red = _dot(pst, ones_b)
        dblocks = []
        for blk in range(nblk):
            acc = jnp.zeros((SUBLANES, c), F32)
            for s in range(SUBLANES):
                idx = blk * SUBLANES + s
                acc = jnp.where(lane_c == idx, red[idx * SUBLANES:(idx + 1) * SUBLANES, :], acc)
            dblocks.append(acc)
        att = att + jnp.concatenate(dblocks, axis=0)

        out = inter + _dot(att.astype(BF16), vb)
        out = _rms(out, gw, NORM_EPS)
        gf = g_ref[0, :, sl]
        o_ref[0, :, sl] = (out * (gf * _sigmoid(gf))).astype(o_ref.dtype)

        bl = b[c - 1:c, :]
        kdl = (kk * jnp.exp(bl - b)).astype(BF16)
        st_new = st * jnp.exp(bl) + _dot_tn(vb, kdl)
        st_ref[h] = st_new

    @pl.when(n == pl.num_programs(1) - 1)
    def _():
        for h in range(heads):
            sn_ref[0, h] = st_ref[h].T


def gla(proj, lb_logits, gnorm_w, s0, layer, name):
    bsz, s, w4 = proj.shape
    w = w4 // 4
    _, heads, dk, dv = s0.shape
    c = min(GLA_CHUNK, s)
    assert s % c == 0 and c % (2 * SUBLANES) == 0 and dk == dv == LANES
    nl = lb_logits.shape[0]
    kern = functools.partial(_gla_kernel, chunk=c, layer=layer, heads=heads, dk=dk)
    return pl.pallas_call(
        kern,
        out_shape=[jax.ShapeDtypeStruct((bsz, s, w), BF16), jax.ShapeDtypeStruct(s0.shape, F32)],
        grid=(bsz, s // c),
        in_specs=[pl.BlockSpec((1, c, w), lambda b, n: (b, n, 0)),
                  pl.BlockSpec((1, c, w), lambda b, n: (b, n, 1)),
                  pl.BlockSpec((1, c, w), lambda b, n: (b, n, 2)),
                  pl.BlockSpec((1, c, w), lambda b, n: (b, n, 3)),
                  pl.BlockSpec((nl, w), lambda b, n: (0, 0)),
                  pl.BlockSpec((1, dv), lambda b, n: (0, 0)),
                  pl.BlockSpec((1, heads, dk, dv), lambda b, n: (b, 0, 0, 0))],
        out_specs=[pl.BlockSpec((1, c, w), lambda b, n: (b, n, 0)),
                   pl.BlockSpec((1, heads, dk, dv), lambda b, n: (b, 0, 0, 0))],
        scratch_shapes=[pltpu.VMEM((heads, dv, dk), F32)],
        compiler_params=_cparams(("parallel", "arbitrary")),
        name=name,
    )(proj, proj, proj, proj, lb_logits, gnorm_w.reshape(1, dv), s0)


def _gla_steps_kernel(p_ref, lbl_ref, gw_ref, s0_ref, o_ref, sn_ref, *, steps, layer, heads, dk):
    w = heads * dk
    rows = p_ref.shape[1]
    lb_all = _forget_lower_bound(lbl_ref[...], layer)
    gw = gw_ref[...]
    for h in range(heads):
        qf = p_ref[0, :, h * dk:(h + 1) * dk]
        ff = p_ref[0, :, w + h * dk:w + (h + 1) * dk]
        v = p_ref[0, :, 2 * w + h * dk:2 * w + (h + 1) * dk]
        gf = p_ref[0, :, 3 * w + h * dk:3 * w + (h + 1) * dk]
        lb = lb_all[:, h * dk:(h + 1) * dk]
        fg = lb + (1.0 - lb) * _sigmoid(ff)
        qs = qf * _sigmoid(qf)
        stack = jnp.concatenate([qs, fg, 1.0 - fg, jnp.zeros((dk - 3 * rows, dk), F32)], axis=0)
        cols = stack.T
        st = s0_ref[0, h]
        outs = []
        for t in range(steps):
            st = st * cols[:, rows + t:rows + t + 1] + cols[:, 2 * rows + t:2 * rows + t + 1] * v[t:t + 1, :]
            outs.append(jnp.sum(st * cols[:, t:t + 1], axis=0, keepdims=True))
        sn_ref[0, h] = st
        out = jnp.concatenate(outs + [jnp.zeros((rows - steps, dk), F32)], axis=0)
        out = _rms(out, gw, NORM_EPS)
        o_ref[0, :, h * dk:(h + 1) * dk] = out * (gf * _sigmoid(gf))


def gla_steps(proj, lb_logits, gnorm_w, s0, layer, steps, name):
    bsz, rows, w4 = proj.shape
    w = w4 // 4
    _, heads, dk, dv = s0.shape
    assert rows % SUBLANES == 0 and 3 * rows <= dk and dk == dv == LANES
    nl = lb_logits.shape[0]
    return pl.pallas_call(
        functools.partial(_gla_steps_kernel, steps=steps, layer=layer, heads=heads, dk=dk),
        out_shape=[jax.ShapeDtypeStruct((bsz, rows, w), F32), jax.ShapeDtypeStruct(s0.shape, F32)],
        grid=(bsz,),
        in_specs=[pl.BlockSpec((1, rows, w4), lambda b: (b, 0, 0)),
                  pl.BlockSpec((nl, w), lambda b: (0, 0)),
                  pl.BlockSpec((1, dv), lambda b: (0, 0)),
                  pl.BlockSpec((1, heads, dk, dv), lambda b: (b, 0, 0, 0))],
        out_specs=[pl.BlockSpec((1, rows, w), lambda b: (b, 0, 0)),
                   pl.BlockSpec((1, heads, dk, dv), lambda b: (b, 0, 0, 0))],
        compiler_params=_cparams(("parallel",)),
        name=name,
    )(proj, lb_logits, gnorm_w.reshape(1, dv), s0)


def _router_logits(hn, wr_ref):
    if wr_ref.dtype == F32:
        return _dot_f32(hn, wr_ref[...])
    hi = hn.astype(BF16)
    lo = (hn - hi.astype(F32)).astype(BF16)
    return _dot(hi, wr_ref[0]) + (_dot(lo, wr_ref[0]) + _dot(hi, wr_ref[1]))


def _router_kernel(h_ref, nw_ref, wr_ref, gid_ref):
    hn = _rms(h_ref[...], nw_ref[...], NORM_EPS)
    logits = _router_logits(hn, wr_ref)
    lane = lax.broadcasted_iota(jnp.int32, logits.shape, 1)
    isg = lane < MOE_GROUPS
    pgl = jnp.where(isg, logits, -jnp.inf)
    m = jnp.max(pgl, axis=-1, keepdims=True)
    gidx = jnp.min(jnp.where(isg & (pgl == m), lane, LANES), axis=-1, keepdims=True)
    gid_ref[...] = jnp.broadcast_to(gidx, gid_ref.shape)


def moe_router(h, nw, wr, name):
    t, d = h.shape
    tm = _row_tile(t, 512)
    return pl.pallas_call(
        _router_kernel,
        out_shape=jax.ShapeDtypeStruct((t, LANES), jnp.int32),
        grid=(t // tm,),
        in_specs=[pl.BlockSpec((tm, d), lambda i: (i, 0)),
                  pl.BlockSpec((1, d), lambda i: (0, 0)),
                  pl.BlockSpec(wr.shape, lambda i: (0,) * wr.ndim)],
        out_specs=pl.BlockSpec((tm, LANES), lambda i: (i, 0)),
        compiler_params=_cparams(("parallel",)),
        name=name,
    )(h, nw.reshape(1, d), wr)


DMA_UNROLL = 8


def _moe_ffn_kernel(tg_ref, nv_ref, ridx_ref, h_hbm, nw_ref, wr_ref, wg_ref, wu_ref, wd_ref, out_hbm,
                    xbuf, obuf, gsem, ssem, *, tm):
    t = pl.program_id(0)
    n_tiles = pl.num_programs(0)
    nv = nv_ref[t]
    g = tg_ref[t]
    slot = t % 2

    def gather_rows(tile, sl):
        def body(i, carry):
            for u in range(DMA_UNROLL):
                r = i * DMA_UNROLL + u
                pltpu.make_async_copy(h_hbm.at[pl.ds(ridx_ref[tile * tm + r], 1)],
                                      xbuf.at[sl, pl.ds(r, 1)], gsem.at[sl]).start()
            return carry
        lax.fori_loop(0, tm // DMA_UNROLL, body, 0)

    def scatter_row(r):
        pltpu.make_async_copy(obuf.at[pl.ds(r, 1)], out_hbm.at[pl.ds(ridx_ref[t * tm + r], 1)], ssem).start()

    def wait_scatter(count):
        groups = count // SUBLANES

        @pl.when(groups > 0)
        def _():
            n = pl.multiple_of(groups * SUBLANES, SUBLANES)
            pltpu.make_async_copy(obuf.at[pl.ds(0, n)], out_hbm.at[pl.ds(0, n)], ssem).wait()

        def one(r, carry):
            pltpu.make_async_copy(obuf.at[pl.ds(0, 1)], out_hbm.at[pl.ds(0, 1)], ssem).wait()
            return carry
        lax.fori_loop(groups * SUBLANES, count, one, 0)

    @pl.when(t == 0)
    def _():
        gather_rows(t, slot)

    nxt = jnp.minimum(t + 1, n_tiles - 1)
    has_next = (t + 1 < n_tiles) & (nv_ref[nxt] > 0)

    @pl.when(has_next)
    def _():
        gather_rows(t + 1, 1 - slot)

    @pl.when(nv > 0)
    def _():
        pltpu.make_async_copy(h_hbm.at[pl.ds(0, tm)], xbuf.at[slot], gsem.at[slot]).wait()
        x = xbuf[slot]
        hn = _rms(x, nw_ref[...], NORM_EPS)
        logits = _router_logits(hn, wr_ref)
        lane = lax.broadcasted_iota(jnp.int32, logits.shape, 1)
        isg = lane < MOE_GROUPS
        m = jnp.max(jnp.where(isg, logits, -jnp.inf), axis=-1, keepdims=True)
        e = jnp.where(isg, jnp.exp(jnp.where(isg, logits - m, 0.0)), 0.0)
        gprob = jnp.sum(jnp.where(lane == g, e, 0.0), axis=-1, keepdims=True) / jnp.sum(e, axis=-1, keepdims=True)
        lo = MOE_GROUPS + MOE_EPG * g
        inb = (lane >= lo) & (lane < lo + MOE_EPG)
        sv = jnp.where(inb, logits, -jnp.inf)
        m1 = jnp.max(sv, axis=-1, keepdims=True)
        i1 = jnp.min(jnp.where(inb & (sv == m1), lane, LANES), axis=-1, keepdims=True)
        inb2 = inb & (lane != i1)
        sv2 = jnp.where(inb2, logits, -jnp.inf)
        m2 = jnp.max(sv2, axis=-1, keepdims=True)
        i2 = jnp.min(jnp.where(inb2 & (sv2 == m2), lane, LANES), axis=-1, keepdims=True)
        e2 = jnp.exp(m2 - m1)
        w1 = gprob / (1.0 + e2)
        w2 = gprob * e2 / (1.0 + e2)
        gate = jnp.where(lane == i1, w1, jnp.where(lane == i2, w2, 0.0))

        precise = wg_ref.dtype == F32
        hb = hn if precise else hn.astype(BF16)
        dot = _dot_f32 if precise else _dot
        y = jnp.zeros(x.shape, F32)
        for ex in range(MOE_EPG):
            ge = jnp.sum(jnp.where(lane == lo + ex, gate, 0.0), axis=-1, keepdims=True)
            a = dot(hb, wg_ref[ex])
            u = dot(hb, wu_ref[ex])
            hid = a * _sigmoid(a) * u * ge
            y = y + dot(hid if precise else hid.astype(BF16), wd_ref[ex])
        @pl.when(t > 0)
        def _():
            wait_scatter(nv_ref[jnp.maximum(t - 1, 0)])

        obuf[...] = x + y

        nfull = nv // DMA_UNROLL

        def issue_out(i, carry):
            for u in range(DMA_UNROLL):
                scatter_row(i * DMA_UNROLL + u)
            return carry

        def issue_out_tail(r, carry):
            scatter_row(r)
            return carry

        lax.fori_loop(0, nfull, issue_out, 0)
        lax.fori_loop(nfull * DMA_UNROLL, nv, issue_out_tail, 0)

        @pl.when(jnp.logical_not(has_next))
        def _():
            wait_scatter(nv)


def _moe_plan(gidx, t, tm):
    ng = MOE_GROUPS
    n_tiles = t // tm + ng
    counts = jnp.sum((gidx[:, None] == jnp.arange(ng)[None, :]).astype(jnp.int32), axis=0)
    tiles_g = (counts + tm - 1) // tm
    tiles_end = jnp.cumsum(tiles_g)
    first_tile = tiles_end - tiles_g
    start_g = jnp.cumsum(counts) - counts
    used = tiles_end[-1]
    perm = jnp.argsort(gidx, stable=True).astype(jnp.int32)
    tile_ids = jnp.arange(n_tiles, dtype=jnp.int32)
    tg = jnp.sum((tile_ids[:, None] >= tiles_end[None, :]).astype(jnp.int32), axis=1)
    last_g = jnp.sum((used - 1 >= tiles_end).astype(jnp.int32))
    tg = jnp.where(tile_ids < used, jnp.minimum(tg, ng - 1), last_g).astype(jnp.int32)
    j = tile_ids - first_tile[tg]
    nv = jnp.clip(counts[tg] - j * tm, 0, tm)
    nv = jnp.where(tile_ids < used, nv, 0).astype(jnp.int32)
    src0 = start_g[tg] + j * tm
    src = src0[:, None] + jnp.arange(tm, dtype=jnp.int32)[None, :]
    ok = jnp.arange(tm, dtype=jnp.int32)[None, :] < nv[:, None]
    src = jnp.where(ok, src, src0[:, None])
    ridx = perm[jnp.clip(src, 0, t - 1)].reshape(-1).astype(jnp.int32)
    return tg, nv, ridx


def moe_layer(h, nw, wr, wg, wu, wd, name):
    t, d = h.shape
    tm = _row_tile(t, 256)
    assert tm % DMA_UNROLL == 0
    gid = moe_router(h, nw, wr, name + "_router")
    tg, nv, ridx = _moe_plan(gid[:, 0], t, tm)
    n_tiles = t // tm + MOE_GROUPS
    fdim = wg.shape[2]
    grid_spec = pltpu.PrefetchScalarGridSpec(
        num_scalar_prefetch=3,
        grid=(n_tiles,),
        in_specs=[pl.BlockSpec(memory_space=pl.ANY),
                  pl.BlockSpec((1, d), lambda i, tg, nv, ri: (0, 0)),
                  pl.BlockSpec(wr.shape, lambda i, tg, nv, ri: (0,) * wr.ndim),
                  pl.BlockSpec((MOE_EPG, d, fdim), lambda i, tg, nv, ri: (tg[i], 0, 0)),
                  pl.BlockSpec((MOE_EPG, d, fdim), lambda i, tg, nv, ri: (tg[i], 0, 0)),
                  pl.BlockSpec((MOE_EPG, fdim, d), lambda i, tg, nv, ri: (tg[i], 0, 0))],
        out_specs=pl.BlockSpec(memory_space=pl.ANY),
        scratch_shapes=[pltpu.VMEM((2, tm, d), F32), pltpu.VMEM((tm, d), F32),
                        pltpu.SemaphoreType.DMA((2,)), pltpu.SemaphoreType.DMA],
    )
    return pl.pallas_call(
        functools.partial(_moe_ffn_kernel, tm=tm),
        out_shape=jax.ShapeDtypeStruct((t, d), F32),
        grid_spec=grid_spec,
        compiler_params=_cparams(("arbitrary",)),
        name=name + "_ffn",
    )(tg, nv, ridx, h, nw.reshape(1, d), wr, wg, wu, wd)


def _lambda_value(lq1, lk1, lq2, lk2, lam_init):
    s1 = jnp.sum(lq1[...] * lk1[...], axis=-1, keepdims=True)
    s2 = jnp.sum(lq2[...] * lk2[...], axis=-1, keepdims=True)
    return jnp.exp(s1) - jnp.exp(s2) + lam_init


ALIBI_SPLIT = 16


def _attn_prompt_kernel(slopes_ref, q_ref, k_ref, v_ref, kf_ref, lq1, lk1, lq2, lk2, sw_ref, o_ref,
                        m_sc, l_sc, acc_sc, *, blk, lam_init):
    h = pl.program_id(1)
    s_len, hd = q_ref.shape[1], q_ref.shape[2]
    nq = s_len // blk
    rows = 2 * blk
    slope = slopes_ref[h]
    lam = _lambda_value(lq1, lk1, lq2, lk2, lam_init)
    lane = lax.broadcasted_iota(jnp.int32, (blk, hd), 1)
    r = lax.broadcasted_iota(jnp.int32, (rows, blk), 0)
    c = lax.broadcasted_iota(jnp.int32, (rows, blk), 1)
    causal = jnp.where(r >= blk, r - blk, r) >= c
    qfeat = jnp.where(lax.broadcasted_iota(jnp.int32, (rows, hd), 1) < 2, 1.0, 0.0).astype(BF16)
    kfeat = kf_ref[0]

    for qi in range(nq):
        q = q_ref[0, qi * blk:(qi + 1) * blk, :]
        zero = jnp.zeros_like(q)
        qa = jnp.concatenate([jnp.concatenate([jnp.where(lane < hd // 2, q, zero),
                                               jnp.where(lane >= hd // 2, q, zero)], axis=0), qfeat], axis=1)
        m_sc[...] = jnp.full(m_sc.shape, NEG_BIG, F32)
        l_sc[...] = jnp.zeros(l_sc.shape, F32)
        acc_sc[...] = jnp.zeros(acc_sc.shape, F32)

        def block(ki, masked, qa=qa):
            start = ki * blk if isinstance(ki, int) else pl.multiple_of(ki * blk, blk)
            ka = jnp.concatenate([k_ref[0, pl.ds(start, blk), :], kfeat], axis=1)
            s = _dot_nt(qa, ka)
            if masked:
                s = jnp.where(causal, s, NEG_BIG)
            off = slope * lax.convert_element_type(ki * blk, F32)
            m_prev = m_sc[...]
            m_new = jnp.maximum(m_prev, jnp.max(s, axis=-1, keepdims=True) + off)
            shift = m_new - off
            p = jnp.exp(s - jnp.concatenate([shift] * (blk // LANES), axis=1))
            alpha = jnp.exp(m_prev - m_new)
            l_sc[...] = alpha * l_sc[...] + jnp.sum(p, axis=-1, keepdims=True)
            acc_sc[...] = alpha * acc_sc[...] + _dot(p.astype(BF16), v_ref[0, pl.ds(start, blk), :])
            m_sc[...] = m_new

        if qi > 0:
            def body(ki, carry):
                block(ki, False)
                return carry
            lax.fori_loop(0, qi, body, 0)
        block(qi, True)

        o = acc_sc[...] / l_sc[...]
        out = o[:blk] - lam * o[blk:]
        out = _rms(out, sw_ref[...], SUBLN_EPS) * (1.0 - lam_init)
        o_ref[0, qi * blk:(qi + 1) * blk, :] = out.astype(o_ref.dtype)


def attn_prompt(q, k, v, slopes, lams, subln_w, lam_init, name):
    bsz, s, w = q.shape
    heads = slopes.shape[0]
    hd = w // heads
    assert hd == LANES and 8 % heads == 0
    blk = min(256, s)
    assert s % blk == 0 and blk % LANES == 0 and blk <= ALIBI_SPLIT * ALIBI_SPLIT
    kpos = jnp.arange(blk)
    feat = jnp.zeros((blk, hd), F32).at[:, 0].set((kpos // ALIBI_SPLIT * ALIBI_SPLIT).astype(F32))
    feat = feat.at[:, 1].set((kpos % ALIBI_SPLIT).astype(F32))
    kfeat = (slopes[:, None, None] * feat[None]).astype(BF16)

    lam_specs = [pl.BlockSpec(lams[0].shape, lambda b, h: (0, 0)) for _ in range(4)]
    return pl.pallas_call(
        functools.partial(_attn_prompt_kernel, blk=blk, lam_init=lam_init),
        out_shape=jax.ShapeDtypeStruct((bsz, s, w), BF16),
        grid=(bsz, heads),
        in_specs=[pl.BlockSpec(memory_space=pltpu.SMEM),
                  pl.BlockSpec((1, s, hd), lambda b, h: (b, 0, h)),
                  pl.BlockSpec((1, s, hd), lambda b, h: (b, 0, h)),
                  pl.BlockSpec((1, s, hd), lambda b, h: (b, 0, h)),
                  pl.BlockSpec((1, blk, hd), lambda b, h: (h, 0, 0))] + lam_specs
                 + [pl.BlockSpec((1, hd), lambda b, h: (0, 0))],
        out_specs=pl.BlockSpec((1, s, hd), lambda b, h: (b, 0, h)),
        scratch_shapes=[pltpu.VMEM((2 * blk, LANES), F32), pltpu.VMEM((2 * blk, LANES), F32),
                        pltpu.VMEM((2 * blk, hd), F32)],
        compiler_params=_cparams(("parallel", "parallel")),
        name=name,
    )(slopes, q, k, v, kfeat, *lams, subln_w.reshape(1, hd))


PAGES_PER_STEP = 8


def _attn_sample_kernel(pt_ref, q_ref, *refs, heads, steps, n_pages, pps, lam_init):
    ck_refs, cv_refs = refs[:pps], refs[pps:2 * pps]
    kn_ref, vn_ref, lq1, lk1, lq2, lk2, sw_ref, o_ref, m_sc, l_sc, acc_sc = refs[2 * pps:]
    p = pl.program_id(1)
    hd = sw_ref.shape[-1]
    hrows = 2 * steps
    nrow = heads * hrows
    past = n_pages * PAGE_SIZE
    n_steps = n_pages // pps

    @pl.when(p == 0)
    def _():
        m_sc[...] = jnp.full(m_sc.shape, NEG_BIG, F32)
        l_sc[...] = jnp.zeros(l_sc.shape, F32)
        acc_sc[...] = jnp.zeros(acc_sc.shape, F32)

    r = lax.broadcasted_iota(jnp.int32, (nrow, PAGE_SIZE), 0)
    col = lax.broadcasted_iota(jnp.int32, (nrow, PAGE_SIZE), 1)
    rt = r % steps
    slope = jnp.exp2(-8.0 * (r // hrows + 1).astype(F32) / heads)

    def head_rows(ref, hh):
        return ref[0, pl.ds(hh, PAGE_SIZE, stride=heads), :].astype(BF16)

    def scores(k_ref):
        return jnp.concatenate([_dot_nt(q_ref[0, hh], head_rows(k_ref, hh)) for hh in range(heads)],
                               axis=0)

    def update(s_pages, v_refs):
        s = s_pages[0] if len(s_pages) == 1 else jnp.concatenate(s_pages, axis=1)
        m_prev = m_sc[...]
        m_new = jnp.maximum(m_prev, jnp.max(s, axis=-1, keepdims=True))
        pe = jnp.exp(s - m_new)
        corr = jnp.exp(m_prev - m_new)
        l_sc[...] = l_sc[...] * corr + jnp.sum(pe, axis=-1, keepdims=True)
        acc = acc_sc[...] * corr
        for j, v_ref in enumerate(v_refs):
            pj = pe[:, j * PAGE_SIZE:(j + 1) * PAGE_SIZE].astype(BF16)
            acc = acc + jnp.concatenate(
                [_dot(pj[hh * hrows:(hh + 1) * hrows], head_rows(v_ref, hh)) for hh in range(heads)],
                axis=0)
        acc_sc[...] = acc
        m_sc[...] = m_new

    @pl.when(p < n_steps)
    def _():
        s_pages = []
        for j in range(pps):
            dist = past + rt - ((p * pps + j) * PAGE_SIZE + col)
            s_pages.append(scores(ck_refs[j]) - slope * dist.astype(F32))
        update(s_pages, cv_refs)

    @pl.when(p == n_steps)
    def _():
        dist = rt - col
        s = jnp.where((dist >= 0) & (col < steps), scores(kn_ref) - slope * dist.astype(F32), NEG_BIG)
        update([s], [vn_ref])
        od = acc_sc[...] / l_sc[...]
        lam = _lambda_value(lq1, lk1, lq2, lk2, lam_init)
        out = od - lam * pltpu.roll(od, nrow - steps, 0)
        out = _rms(out, sw_ref[...], SUBLN_EPS) * (1.0 - lam_init)
        o_ref[0] = out.astype(o_ref.dtype)


def attn_sample(qh, cache_k, cache_v, knew, vnew, page_table, lams, subln_w, lam_init, steps, name):
    db, heads, hrows, hd = qh.shape
    n_pages = page_table.shape[1]
    pps = math.gcd(PAGES_PER_STEP, n_pages)
    n_steps = n_pages // pps
    pt = page_table.reshape(-1).astype(jnp.int32)
    cache_k, cache_v, knew, vnew = [a.reshape(a.shape[0], PAGE_SIZE * heads, hd) for a in (cache_k, cache_v, knew, vnew)]

    def page_map(j):
        return lambda b, p, pt: (pt[b * n_pages + jnp.minimum(p, n_steps - 1) * pps + j], 0, 0)

    page_specs = [pl.BlockSpec((1, PAGE_SIZE * heads, hd), page_map(j)) for j in range(pps)]
    new_spec = pl.BlockSpec((1, PAGE_SIZE * heads, hd), lambda b, p, pt: (b, 0, 0))
    lam_specs = [pl.BlockSpec(lams[0].shape, lambda b, p, pt: (0, 0)) for _ in range(4)]
    grid_spec = pltpu.PrefetchScalarGridSpec(
        num_scalar_prefetch=1,
        grid=(db, n_steps + 1),
        in_specs=[pl.BlockSpec((1, heads, hrows, hd), lambda b, p, pt: (b, 0, 0, 0))] + page_specs + page_specs
                 + [new_spec, new_spec] + lam_specs + [pl.BlockSpec((1, hd), lambda b, p, pt: (0, 0))],
        out_specs=pl.BlockSpec((1, heads * hrows, hd), lambda b, p, pt: (b, 0, 0)),
        scratch_shapes=[pltpu.VMEM((heads * hrows, 1), F32), pltpu.VMEM((heads * hrows, 1), F32),
                        pltpu.VMEM((heads * hrows, hd), F32)],
    )
    return pl.pallas_call(
        functools.partial(_attn_sample_kernel, heads=heads, steps=steps, n_pages=n_pages, pps=pps,
                          lam_init=lam_init),
        out_shape=jax.ShapeDtypeStruct((db, heads * hrows, hd), F32),
        grid_spec=grid_spec,
        compiler_params=_cparams(("parallel", "arbitrary")),
        name=name,
    )(pt, qh, *([cache_k] * pps), *([cache_v] * pps), knew, vnew, *lams, subln_w.reshape(1, hd))


def _router_weights(w_group, w_sub):
    d = w_group.shape[0]
    ws = jnp.transpose(w_sub, (1, 0, 2)).reshape(d, MOE_GROUPS * MOE_EPG)
    pad = jnp.zeros((d, LANES - MOE_GROUPS - MOE_GROUPS * MOE_EPG), F32)
    return jnp.concatenate([w_group.astype(F32), ws.astype(F32), pad], axis=1)


def _head_queries(q, heads, steps):
    db = q.shape[0]
    dh = q.shape[-1] // heads // 2
    q5 = q.reshape(db, steps, heads, 2, dh)
    eye_c = jnp.eye(2, dtype=q.dtype)
    qh = jnp.einsum('bthcd,ce->bhcted', q5, eye_c)
    return qh.reshape(db, heads, 2 * steps, 2 * dh)


def _trunk(x, s0_all, paged, wts, tag):
    bsz, s, d = x.shape
    t = bsz * s
    depth = wts['norm_mix_w'].shape[0]
    n_a = wts['a_w_in'].shape[0]
    heads_a = s0_all.shape[2]
    heads_b = wts['slopes'].shape[0]
    hd = d // heads_b
    h = x.reshape(t, d)
    states = []
    k_f = v_f = k_b = v_b = None
    precise = paged is not None
    mm = wts['f32'] if precise else wts['bf16']
    for l in range(depth):
        if l < n_a:
            (proj,) = norm_matmul(h, wts['norm_mix_w'][l], mm['a_w_in'][l], [(0, 4 * d, 1.0, 1)], [F32],
                                  f"{tag}_l{l}_inproj")
            proj = proj.reshape(bsz, s, 4 * d)
            if s >= GLA_CHUNK:
                o, s_new = gla(proj, wts['a_lb_logits'], wts['a_gnorm_w'][l], s0_all[l], l, f"{tag}_l{l}_gla")
            else:
                rows = -(-s // SUBLANES) * SUBLANES
                proj = jnp.pad(proj, ((0, 0), (0, rows - s), (0, 0)))
                o, s_new = gla_steps(proj, wts['a_lb_logits'], wts['a_gnorm_w'][l], s0_all[l], l, s,
                                     f"{tag}_l{l}_gla")
                o = o[:, :s]
            states.append(s_new)
            h = matmul_residual(o.reshape(t, d), mm['a_w_o'][l], h, f"{tag}_l{l}_oproj")
        else:
            j = l - n_a
            lam_init = 0.8 - 0.6 * math.exp(-0.3 * l)
            lams = [wts[nm][j].reshape(1, -1) for nm in ('b_lambda_q1', 'b_lambda_k1', 'b_lambda_q2', 'b_lambda_k2')]
            (q,) = norm_matmul(h, wts['norm_mix_w'][l], mm['b_w_q'][j], [(0, d, (hd // 2) ** -0.5, 1)],
                               [F32 if precise else BF16], f"{tag}_l{l}_qproj")
            if paged is None:
                o = attn_prompt(q.reshape(bsz, s, d), k_b.reshape(bsz, s, d), v_b.reshape(bsz, s, d),
                                wts['slopes'], lams, wts['b_subln_w'][j], lam_init, f"{tag}_l{l}_attn")
                o = o.reshape(t, d)
            else:
                cache_k, cache_v, page_table = paged
                qh = _head_queries(q.reshape(bsz, s, d), heads_b, s).astype(BF16)
                padn = ((0, 0), (0, PAGE_SIZE - s), (0, 0), (0, 0))
                o = attn_sample(qh, cache_k, cache_v, jnp.pad(k_f.reshape(bsz, s, heads_b, hd), padn),
                                jnp.pad(v_f.reshape(bsz, s, heads_b, hd), padn), page_table, lams,
                                wts['b_subln_w'][j], lam_init, s, f"{tag}_l{l}_attn")
                o = o.reshape(bsz, heads_b, 2, s, hd)[:, :, 0].transpose(0, 2, 1, 3).reshape(t, d)
            h = matmul_residual(o, mm['b_w_o'][j], h, f"{tag}_l{l}_oproj")
        h = moe_layer(h, wts['norm_ffn_w'][l], wts['router_w_f32' if precise else 'router_w_split'][l],
                      mm['moe_w_gate'][l], mm['moe_w_up'][l], mm['moe_w_down'][l], f"{tag}_l{l}_moe")
        if l == n_a - 1:
            k_f, k_b, v_f, v_b = norm_matmul(h, wts['kv_norm_w'], mm['w_kv'], [(0, d, 1.0, 2), (d, d, 1.0, 2)],
                                            [F32, BF16, F32, BF16], f"{tag}_kvproj")
    y = rmsnorm_rows(h, wts['final_norm_w'], f"{tag}_final_norm")
    return (y.reshape(bsz, s, d), jnp.stack(states),
            k_f.reshape(bsz, s, heads_b, hd), v_f.reshape(bsz, s, heads_b, hd))


def kernel(x_prompt, x_sample, state_hgrn, cache_k, cache_v, page_table, norm_mix_w, norm_ffn_w, a_w_in,
           a_lb_logits, a_gnorm_w, a_w_o, kv_norm_w, w_kv, b_w_q, b_lambda_q1, b_lambda_k1, b_lambda_q2,
           b_lambda_k2, b_subln_w, b_w_o, moe_w_group, moe_w_sub, moe_w_gate, moe_w_up, moe_w_down,
           final_norm_w):
    depth = norm_mix_w.shape[0]
    heads_b = cache_k.shape[2]
    mats = dict(a_w_in=a_w_in, a_w_o=a_w_o, w_kv=w_kv, b_w_q=b_w_q, b_w_o=b_w_o,
                moe_w_gate=moe_w_gate, moe_w_up=moe_w_up, moe_w_down=moe_w_down)
    router_w = jnp.stack([_router_weights(moe_w_group[l], moe_w_sub[l]) for l in range(depth)])
    router_hi = router_w.astype(BF16)
    router_lo = (router_w - router_hi.astype(F32)).astype(BF16)
    wts = dict(
        norm_mix_w=norm_mix_w, norm_ffn_w=norm_ffn_w, a_lb_logits=a_lb_logits, a_w_in=a_w_in,
        a_gnorm_w=a_gnorm_w, kv_norm_w=kv_norm_w, b_lambda_q1=b_lambda_q1, b_lambda_k1=b_lambda_k1,
        b_lambda_q2=b_lambda_q2, b_lambda_k2=b_lambda_k2, b_subln_w=b_subln_w,
        router_w_f32=router_w, router_w_split=jnp.stack([router_hi, router_lo], axis=1),
        final_norm_w=final_norm_w,
        slopes=2.0 ** (-8.0 * jnp.arange(1, heads_b + 1, dtype=F32) / heads_b),
        f32=mats, bf16={k: v.astype(BF16) for k, v in mats.items()},
    )
    n_a = a_w_in.shape[0]
    bsz = x_prompt.shape[0]
    s0_prompt = jnp.zeros((n_a, bsz) + state_hgrn.shape[2:], F32)
    y_p, st_p, k_p, v_p = _trunk(x_prompt, s0_prompt, None, wts, "p")
    paged = (cache_k, cache_v, page_table)
    y_s, st_s, k_s, v_s = _trunk(x_sample, state_hgrn, paged, wts, "s")
    return (y_p, y_s, st_p, st_s, k_p, v_p, k_s, v_s)
```

```python
import functools
import math

import jax
import jax.numpy as jnp
from jax import lax
from jax.experimental import pallas as pl
from jax.experimental.pallas import tpu as pltpu

F32 = jnp.float32
BF16 = jnp.bfloat16
HIGHEST = lax.Precision.HIGHEST

LANES = 128
SUBLANES = 8
VMEM_LIMIT_BYTES = 48 * 1024 * 1024

NORM_EPS = 1e-6
SUBLN_EPS = 1e-5
NEG_BIG = -1e30
GLA_CHUNK = 128
MOE_GROUPS = 4
MOE_EPG = 4
PAGE_SIZE = 128


def _cparams(sem):
    return pltpu.CompilerParams(dimension_semantics=sem, vmem_limit_bytes=VMEM_LIMIT_BYTES)


def _rms(x, w, eps):
    var = jnp.mean(x * x, axis=-1, keepdims=True)
    return x * lax.rsqrt(var + eps) * w


def _sigmoid(x):
    return 1.0 / (1.0 + jnp.exp(-x))


def _dot(a, b):
    return jnp.dot(a, b, preferred_element_type=F32)


def _dot_f32(a, b):
    return jnp.dot(a, b, precision=HIGHEST, preferred_element_type=F32)


def _dot_nt(a, b):
    return lax.dot_general(a, b, (((1,), (1,)), ((), ())), preferred_element_type=F32)


def _dot_tn(a, b):
    return lax.dot_general(a, b, (((0,), (0,)), ((), ())), preferred_element_type=F32)


def _row_tile(t, pref):
    tm = min(pref, t)
    assert t % tm == 0, (t, tm)
    return tm


def _norm_matmul_kernel(x_ref, nw_ref, w_ref, *out_refs, groups, nchunk):
    hn = _rms(x_ref[...], nw_ref[...], NORM_EPS)
    precise = w_ref.dtype == F32
    if not precise:
        hn = hn.astype(BF16)
    k = 0
    for (c0, width, scale, n_out) in groups:
        refs = out_refs[k:k + n_out]
        k += n_out
        for j in range(0, width, nchunk):
            wblk = w_ref[:, c0 + j:c0 + j + nchunk]
            acc = _dot_f32(hn, wblk) if precise else _dot(hn, wblk)
            if scale != 1.0:
                acc = acc * scale
            for r in refs:
                r[:, j:j + nchunk] = acc.astype(r.dtype)


def norm_matmul(x, nw, w, groups, out_dtypes, name):
    t, d = x.shape
    tm = _row_tile(t, 256)
    widths = []
    for (c0, width, scale, n_out) in groups:
        widths += [width] * n_out
    nchunk = min(512, min(widths))
    out_shape = [jax.ShapeDtypeStruct((t, wd), dt) for wd, dt in zip(widths, out_dtypes)]
    return pl.pallas_call(
        functools.partial(_norm_matmul_kernel, groups=tuple(groups), nchunk=nchunk),
        out_shape=out_shape,
        grid=(t // tm,),
        in_specs=[pl.BlockSpec((tm, d), lambda i: (i, 0)),
                  pl.BlockSpec((1, d), lambda i: (0, 0)),
                  pl.BlockSpec(w.shape, lambda i: (0, 0))],
        out_specs=[pl.BlockSpec((tm, wd), lambda i: (i, 0)) for wd in widths],
        compiler_params=_cparams(("parallel",)),
        name=name,
    )(x, nw.reshape(1, d), w)


def _matmul_res_kernel(x_ref, w_ref, r_ref, o_ref):
    if w_ref.dtype == F32:
        o_ref[...] = r_ref[...] + _dot_f32(x_ref[...], w_ref[...])
    else:
        o_ref[...] = r_ref[...] + _dot(x_ref[...], w_ref[...])


def matmul_residual(x, w, res, name):
    t, kdim = x.shape
    n = w.shape[1]
    tm = _row_tile(t, 512)
    return pl.pallas_call(
        _matmul_res_kernel,
        out_shape=jax.ShapeDtypeStruct((t, n), F32),
        grid=(t // tm,),
        in_specs=[pl.BlockSpec((tm, kdim), lambda i: (i, 0)),
                  pl.BlockSpec((kdim, n), lambda i: (0, 0)),
                  pl.BlockSpec((tm, n), lambda i: (i, 0))],
        out_specs=pl.BlockSpec((tm, n), lambda i: (i, 0)),
        compiler_params=_cparams(("parallel",)),
        name=name,
    )(x, w, res)


def _rmsnorm_kernel(x_ref, w_ref, o_ref):
    o_ref[...] = _rms(x_ref[...], w_ref[...], NORM_EPS)


def rmsnorm_rows(x, w, name):
    t, d = x.shape
    tm = _row_tile(t, 512)
    return pl.pallas_call(
        _rmsnorm_kernel,
        out_shape=jax.ShapeDtypeStruct((t, d), F32),
        grid=(t // tm,),
        in_specs=[pl.BlockSpec((tm, d), lambda i: (i, 0)), pl.BlockSpec((1, d), lambda i: (0, 0))],
        out_specs=pl.BlockSpec((tm, d), lambda i: (i, 0)),
        compiler_params=_cparams(("parallel",)),
        name=name,
    )(x, w.reshape(1, d))


def _forget_lower_bound(logits, layer):
    le = jnp.exp(logits - jnp.max(logits, axis=0, keepdims=True))
    lp = le / jnp.sum(le, axis=0, keepdims=True)
    lb = jnp.zeros((1, logits.shape[1]), F32)
    for li in range(1, layer + 1):
        lb = lb + lp[li:li + 1, :]
    return lb


def _gla_kernel(q_ref, f_ref, i_ref, g_ref, lbl_ref, gw_ref, s0_ref, o_ref, sn_ref, st_ref,
                *, chunk, layer, heads, dk):
    n = pl.program_id(1)
    c = chunk

    @pl.when(n == 0)
    def _():
        for h in range(heads):
            st_ref[h] = s0_ref[0, h].T

    lb_all = _forget_lower_bound(lbl_ref[...], layer)

    row = lax.broadcasted_iota(jnp.int32, (c, dk), 0)
    rr = lax.broadcasted_iota(jnp.int32, (c, c), 0)
    cc = lax.broadcasted_iota(jnp.int32, (c, c), 1)
    tri = (rr >= cc).astype(F32)
    ones_b = jnp.ones((dk, c), BF16)
    gw = gw_ref[...]
    nblk = c // SUBLANES
    srow = lax.broadcasted_iota(jnp.int32, (SUBLANES, dk), 0)
    lane_c = lax.broadcasted_iota(jnp.int32, (SUBLANES, c), 1)

    for h in range(heads):
        sl = slice(h * dk, (h + 1) * dk)
        lb = lb_all[:, sl]
        fg = lb + (1.0 - lb) * _sigmoid(f_ref[0, :, sl])
        logf = jnp.log(fg)
        kk = 1.0 - fg
        qf = q_ref[0, :, sl]
        qs = qf * _sigmoid(qf)
        v = i_ref[0, :, sl]
        vb = v.astype(BF16)
        b = lax.dot_general(tri, logf, (((1,), (0,)), ((), ())), precision=HIGHEST,
                            preferred_element_type=F32)
        st = st_ref[h]
        inter = _dot_nt((qs * jnp.exp(b)).astype(BF16), st.astype(BF16))

        att = jnp.zeros((c, c), F32)
        half = c // 2
        while half >= SUBLANES:
            lower = ((row // half) % 2) == 1
            mids = []
            for blk in range(c // (2 * half)):
                mid = b[blk * 2 * half + half - 1: blk * 2 * half + half, :]
                mids.append(jnp.broadcast_to(mid, (2 * half, dk)))
            bmid = mids[0] if len(mids) == 1 else jnp.concatenate(mids, axis=0)
            ex = jnp.exp(jnp.where(lower, b - bmid, bmid - b))
            qd = jnp.where(lower, qs * ex, 0.0).astype(BF16)
            kd = jnp.where(lower, 0.0, kk * ex).astype(BF16)
            a = _dot_nt(qd, kd)
            same = (rr // (2 * half)) == (cc // (2 * half))
            att = att + jnp.where(same, a, 0.0)
            half //= 2

        pieces = []
        for blk in range(nblk):
            r0 = blk * SUBLANES
            bb = b[r0:r0 + SUBLANES, :]
            qb = qs[r0:r0 + SUBLANES, :]
            kb = kk[r0:r0 + SUBLANES, :]
            for s in range(SUBLANES):
                keep = srow >= s
                dec = jnp.exp(jnp.where(keep, bb - bb[s:s + 1, :], 0.0))
                pieces.append(jnp.where(keep, qb * kb[s:s + 1, :] * dec, 0.0).astype(BF16))
        pst = jnp.concatenate(pieces, axis=0)
        red = _dot(pst, ones_b)
        dblocks = []
        for blk in range(nblk):
            acc = jnp.zeros((SUBLANES, c), F32)
            for s in range(SUBLANES):
                idx = blk * SUBLANES + s
                acc = jnp.where(lane_c == idx, red[idx * SUBLANES:(idx + 1) * SUBLANES, :], acc)
            dblocks.append(acc)
        att = att + jnp.concatenate(dblocks, axis=0)

        out = inter + _dot(att.astype(BF16), vb)
        out = _rms(out, gw, NORM_EPS)
        gf = g_ref[0, :, sl]
        o_ref[0, :, sl] = (out * (gf * _sigmoid(gf))).astype(o_ref.dtype)

        bl = b[c - 1:c, :]
        kdl = (kk * jnp.exp(bl - b)).astype(BF16)
        st_new = st * jnp.exp(bl) + _dot_tn(vb, kdl)
        st_ref[h] = st_new

    @pl.when(n == pl.num_programs(1) - 1)
    def _():
        for h in range(heads):
            sn_ref[0, h] = st_ref[h].T


def gla(proj, lb_logits, gnorm_w, s0, layer, name):
    bsz, s, w4 = proj.shape
    w = w4 // 4
    _, heads, dk, dv = s0.shape
    c = min(GLA_CHUNK, s)
    assert s % c == 0 and c % (2 * SUBLANES) == 0 and dk == dv == LANES
    nl = lb_logits.shape[0]
    kern = functools.partial(_gla_kernel, chunk=c, layer=layer, heads=heads, dk=dk)
    return pl.pallas_call(
        kern,
        out_shape=[jax.ShapeDtypeStruct((bsz, s, w), BF16), jax.ShapeDtypeStruct(s0.shape, F32)],
        grid=(bsz, s // c),
        in_specs=[pl.BlockSpec((1, c, w), lambda b, n: (b, n, 0)),
                  pl.BlockSpec((1, c, w), lambda b, n: (b, n, 1)),
                  pl.BlockSpec((1, c, w), lambda b, n: (b, n, 2)),
                  pl.BlockSpec((1, c, w), lambda b, n: (b, n, 3)),
                  pl.BlockSpec((nl, w), lambda b, n: (0, 0)),
                  pl.BlockSpec((1, dv), lambda b, n: (0, 0)),
                  pl.BlockSpec((1, heads, dk, dv), lambda b, n: (b, 0, 0, 0))],
        out_specs=[pl.BlockSpec((1, c, w), lambda b, n: (b, n, 0)),
                   pl.BlockSpec((1, heads, dk, dv), lambda b, n: (b, 0, 0, 0))],
        scratch_shapes=[pltpu.VMEM((heads, dv, dk), F32)],
        compiler_params=_cparams(("parallel", "arbitrary")),
        name=name,
    )(proj, proj, proj, proj, lb_logits, gnorm_w.reshape(1, dv), s0)


def _gla_steps_kernel(p_ref, lbl_ref, gw_ref, s0_ref, o_ref, sn_ref, *, steps, layer, heads, dk):
    w = heads * dk
    rows = p_ref.shape[1]
    lb_all = _forget_lower_bound(lbl_ref[...], layer)
    gw = gw_ref[...]
    for h in range(heads):
        qf = p_ref[0, :, h * dk:(h + 1) * dk]
        ff = p_ref[0, :, w + h * dk:w + (h + 1) * dk]
        v = p_ref[0, :, 2 * w + h * dk:2 * w + (h + 1) * dk]
        gf = p_ref[0, :, 3 * w + h * dk:3 * w + (h + 1) * dk]
        lb = lb_all[:, h * dk:(h + 1) * dk]
        fg = lb + (1.0 - lb) * _sigmoid(ff)
        qs = qf * _sigmoid(qf)
        stack = jnp.concatenate([qs, fg, 1.0 - fg, jnp.zeros((dk - 3 * rows, dk), F32)], axis=0)
        cols = stack.T
        st = s0_ref[0, h]
        outs = []
        for t in range(steps):
            st = st * cols[:, rows + t:rows + t + 1] + cols[:, 2 * rows + t:2 * rows + t + 1] * v[t:t + 1, :]
            outs.append(jnp.sum(st * cols[:, t:t + 1], axis=0, keepdims=True))
        sn_ref[0, h] = st
        out = jnp.concatenate(outs + [jnp.zeros((rows - steps, dk), F32)], axis=0)
        out = _rms(out, gw, NORM_EPS)
        o_ref[0, :, h * dk:(h + 1) * dk] = out * (gf * _sigmoid(gf))


def gla_steps(proj, lb_logits, gnorm_w, s0, layer, steps, name):
    bsz, rows, w4 = proj.shape
    w = w4 // 4
    _, heads, dk, dv = s0.shape
    assert rows % SUBLANES == 0 and 3 * rows <= dk and dk == dv == LANES
    nl = lb_logits.shape[0]
    return pl.pallas_call(
        functools.partial(_gla_steps_kernel, steps=steps, layer=layer, heads=heads, dk=dk),
        out_shape=[jax.ShapeDtypeStruct((bsz, rows, w), F32), jax.ShapeDtypeStruct(s0.shape, F32)],
        grid=(bsz,),
        in_specs=[pl.BlockSpec((1, rows, w4), lambda b: (b, 0, 0)),
                  pl.BlockSpec((nl, w), lambda b: (0, 0)),
                  pl.BlockSpec((1, dv), lambda b: (0, 0)),
                  pl.BlockSpec((1, heads, dk, dv), lambda b: (b, 0, 0, 0))],
        out_specs=[pl.BlockSpec((1, rows, w), lambda b: (b, 0, 0)),
                   pl.BlockSpec((1, heads, dk, dv), lambda b: (b, 0, 0, 0))],
        compiler_params=_cparams(("parallel",)),
        name=name,
    )(proj, lb_logits, gnorm_w.reshape(1, dv), s0)


def _router_logits(hn, wr_ref):
    if wr_ref.dtype == F32:
        return _dot_f32(hn, wr_ref[...])
    hi = hn.astype(BF16)
    lo = (hn - hi.astype(F32)).astype(BF16)
    return _dot(hi, wr_ref[0]) + (_dot(lo, wr_ref[0]) + _dot(hi, wr_ref[1]))


def _router_kernel(h_ref, nw_ref, wr_ref, gid_ref):
    hn = _rms(h_ref[...], nw_ref[...], NORM_EPS)
    logits = _router_logits(hn, wr_ref)
    lane = lax.broadcasted_iota(jnp.int32, logits.shape, 1)
    isg = lane < MOE_GROUPS
    pgl = jnp.where(isg, logits, -jnp.inf)
    m = jnp.max(pgl, axis=-1, keepdims=True)
    gidx = jnp.min(jnp.where(isg & (pgl == m), lane, LANES), axis=-1, keepdims=True)
    gid_ref[...] = jnp.broadcast_to(gidx, gid_ref.shape)


def moe_router(h, nw, wr, name):
    t, d = h.shape
    tm = _row_tile(t, 512)
    return pl.pallas_call(
        _router_kernel,
        out_shape=jax.ShapeDtypeStruct((t, LANES), jnp.int32),
        grid=(t // tm,),
        in_specs=[pl.BlockSpec((tm, d), lambda i: (i, 0)),
                  pl.BlockSpec((1, d), lambda i: (0, 0)),
                  pl.BlockSpec(wr.shape, lambda i: (0,) * wr.ndim)],
        out_specs=pl.BlockSpec((tm, LANES), lambda i: (i, 0)),
        compiler_params=_cparams(("parallel",)),
        name=name,
    )(h, nw.reshape(1, d), wr)


DMA_UNROLL = 8


def _moe_ffn_kernel(tg_ref, nv_ref, ridx_ref, h_hbm, nw_ref, wr_ref, wg_ref, wu_ref, wd_ref, out_hbm,
                    xbuf, obuf, gsem, ssem, *, tm):
    t = pl.program_id(0)
    n_tiles = pl.num_programs(0)
    nv = nv_ref[t]
    g = tg_ref[t]
    slot = t % 2

    def gather_rows(tile, sl):
        def body(i, carry):
            for u in range(DMA_UNROLL):
                r = i * DMA_UNROLL + u
                pltpu.make_async_copy(h_hbm.at[pl.ds(ridx_ref[tile * tm + r], 1)],
                                      xbuf.at[sl, pl.ds(r, 1)], gsem.at[sl]).start()
            return carry
        lax.fori_loop(0, tm // DMA_UNROLL, body, 0)

    def scatter_row(r):
        pltpu.make_async_copy(obuf.at[pl.ds(r, 1)], out_hbm.at[pl.ds(ridx_ref[t * tm + r], 1)], ssem).start()

    def wait_scatter(count):
        groups = count // SUBLANES

        @pl.when(groups > 0)
        def _():
            n = pl.multiple_of(groups * SUBLANES, SUBLANES)
            pltpu.make_async_copy(obuf.at[pl.ds(0, n)], out_hbm.at[pl.ds(0, n)], ssem).wait()

        def one(r, carry):
            pltpu.make_async_copy(obuf.at[pl.ds(0, 1)], out_hbm.at[pl.ds(0, 1)], ssem).wait()
            return carry
        lax.fori_loop(groups * SUBLANES, count, one, 0)

    def wait_gather(sl):
        pltpu.make_async_copy(h_hbm.at[pl.ds(0, tm)], xbuf.at[sl], gsem.at[sl]).wait()

    @pl.when(t == 0)
    def _():
        gather_rows(t, slot)

    nxt = jnp.minimum(t + 1, n_tiles - 1)
    has_next = (t + 1 < n_tiles) & (nv_ref[nxt] > 0)

    @pl.when(nv > 0)
    def _():
        wait_gather(slot)
        x = xbuf[slot]
        for r in range(tm):
            pltpu.make_async_copy(h_hbm.at[pl.ds(ridx_ref[nxt * tm + r], 1)],
                                  xbuf.at[1 - slot, pl.ds(r, 1)], gsem.at[1 - slot]).start()
        hn = _rms(x, nw_ref[...], NORM_EPS)
        logits = _router_logits(hn, wr_ref)
        lane = lax.broadcasted_iota(jnp.int32, logits.shape, 1)
        isg = lane < MOE_GROUPS
        m = jnp.max(jnp.where(isg, logits, -jnp.inf), axis=-1, keepdims=True)
        e = jnp.where(isg, jnp.exp(jnp.where(isg, logits - m, 0.0)), 0.0)
        gprob = jnp.sum(jnp.where(lane == g, e, 0.0), axis=-1, keepdims=True) / jnp.sum(e, axis=-1, keepdims=True)
        lo = MOE_GROUPS + MOE_EPG * g
        inb = (lane >= lo) & (lane < lo + MOE_EPG)
        sv = jnp.where(inb, logits, -jnp.inf)
        m1 = jnp.max(sv, axis=-1, keepdims=True)
        i1 = jnp.min(jnp.where(inb & (sv == m1), lane, LANES), axis=-1, keepdims=True)
        inb2 = inb & (lane != i1)
        sv2 = jnp.where(inb2, logits, -jnp.inf)
        m2 = jnp.max(sv2, axis=-1, keepdims=True)
        i2 = jnp.min(jnp.where(inb2 & (sv2 == m2), lane, LANES), axis=-1, keepdims=True)
        e2 = jnp.exp(m2 - m1)
        w1 = gprob / (1.0 + e2)
        w2 = gprob * e2 / (1.0 + e2)
        gate = jnp.where(lane == i1, w1, jnp.where(lane == i2, w2, 0.0))

        precise = wg_ref.dtype == F32
        hb = hn if precise else hn.astype(BF16)
        dot = _dot_f32 if precise else _dot
        y = jnp.zeros(x.shape, F32)
        for ex in range(MOE_EPG):
            ge = jnp.sum(jnp.where(lane == lo + ex, gate, 0.0), axis=-1, keepdims=True)
            a = dot(hb, wg_ref[ex])
            u = dot(hb, wu_ref[ex])
            hid = a * _sigmoid(a) * u * ge
            y = y + dot(hid if precise else hid.astype(BF16), wd_ref[ex])
        @pl.when(t > 0)
        def _():
            wait_scatter(nv_ref[jnp.maximum(t - 1, 0)])

        obuf[...] = x + y

        nfull = nv // DMA_UNROLL

        def issue_out(i, carry):
            for u in range(DMA_UNROLL):
                scatter_row(i * DMA_UNROLL + u)
            return carry

        def issue_out_tail(r, carry):
            scatter_row(r)
            return carry

        lax.fori_loop(0, nfull, issue_out, 0)
        lax.fori_loop(nfull * DMA_UNROLL, nv, issue_out_tail, 0)

        @pl.when(jnp.logical_not(has_next))
        def _():
            wait_scatter(nv)
            wait_gather(1 - slot)


def _moe_plan(gidx, t, tm):
    ng = MOE_GROUPS
    n_tiles = t // tm + ng
    counts = jnp.sum((gidx[:, None] == jnp.arange(ng)[None, :]).astype(jnp.int32), axis=0)
    tiles_g = (counts + tm - 1) // tm
    tiles_end = jnp.cumsum(tiles_g)
    first_tile = tiles_end - tiles_g
    start_g = jnp.cumsum(counts) - counts
    used = tiles_end[-1]
    perm = jnp.argsort(gidx, stable=True).astype(jnp.int32)
    tile_ids = jnp.arange(n_tiles, dtype=jnp.int32)
    tg = jnp.sum((tile_ids[:, None] >= tiles_end[None, :]).astype(jnp.int32), axis=1)
    last_g = jnp.sum((used - 1 >= tiles_end).astype(jnp.int32))
    tg = jnp.where(tile_ids < used, jnp.minimum(tg, ng - 1), last_g).astype(jnp.int32)
    j = tile_ids - first_tile[tg]
    nv = jnp.clip(counts[tg] - j * tm, 0, tm)
    nv = jnp.where(tile_ids < used, nv, 0).astype(jnp.int32)
    src0 = start_g[tg] + j * tm
    src = src0[:, None] + jnp.arange(tm, dtype=jnp.int32)[None, :]
    ok = jnp.arange(tm, dtype=jnp.int32)[None, :] < nv[:, None]
    src = jnp.where(ok, src, src0[:, None])
    ridx = perm[jnp.clip(src, 0, t - 1)].reshape(-1).astype(jnp.int32)
    return tg, nv, ridx


def moe_layer(h, nw, wr, wg, wu, wd, name):
    t, d = h.shape
    tm = _row_tile(t, 256)
    assert tm % DMA_UNROLL == 0
    gid = moe_router(h, nw, wr, name + "_router")
    tg, nv, ridx = _moe_plan(gid[:, 0], t, tm)
    n_tiles = t // tm + MOE_GROUPS
    fdim = wg.shape[2]
    grid_spec = pltpu.PrefetchScalarGridSpec(
        num_scalar_prefetch=3,
        grid=(n_tiles,),
        in_specs=[pl.BlockSpec(memory_space=pl.ANY),
                  pl.BlockSpec((1, d), lambda i, tg, nv, ri: (0, 0)),
                  pl.BlockSpec(wr.shape, lambda i, tg, nv, ri: (0,) * wr.ndim),
                  pl.BlockSpec((MOE_EPG, d, fdim), lambda i, tg, nv, ri: (tg[i], 0, 0)),
                  pl.BlockSpec((MOE_EPG, d, fdim), lambda i, tg, nv, ri: (tg[i], 0, 0)),
                  pl.BlockSpec((MOE_EPG, fdim, d), lambda i, tg, nv, ri: (tg[i], 0, 0))],
        out_specs=pl.BlockSpec(memory_space=pl.ANY),
        scratch_shapes=[pltpu.VMEM((2, tm, d), F32), pltpu.VMEM((tm, d), F32),
                        pltpu.SemaphoreType.DMA((2,)), pltpu.SemaphoreType.DMA],
    )
    return pl.pallas_call(
        functools.partial(_moe_ffn_kernel, tm=tm),
        out_shape=jax.ShapeDtypeStruct((t, d), F32),
        grid_spec=grid_spec,
        compiler_params=_cparams(("arbitrary",)),
        name=name + "_ffn",
    )(tg, nv, ridx, h, nw.reshape(1, d), wr, wg, wu, wd)


def _lambda_value(lq1, lk1, lq2, lk2, lam_init):
    s1 = jnp.sum(lq1[...] * lk1[...], axis=-1, keepdims=True)
    s2 = jnp.sum(lq2[...] * lk2[...], axis=-1, keepdims=True)
    return jnp.exp(s1) - jnp.exp(s2) + lam_init


ALIBI_SPLIT = 16


def _attn_prompt_kernel(slopes_ref, q_ref, k_ref, v_ref, kf_ref, lq1, lk1, lq2, lk2, sw_ref, o_ref,
                        m_sc, l_sc, acc_sc, *, blk, lam_init):
    h = pl.program_id(1)
    s_len, hd = q_ref.shape[1], q_ref.shape[2]
    nq = s_len // blk
    rows = 2 * blk
    slope = slopes_ref[h]
    lam = _lambda_value(lq1, lk1, lq2, lk2, lam_init)
    lane = lax.broadcasted_iota(jnp.int32, (blk, hd), 1)
    r = lax.broadcasted_iota(jnp.int32, (rows, blk), 0)
    c = lax.broadcasted_iota(jnp.int32, (rows, blk), 1)
    causal = jnp.where(r >= blk, r - blk, r) >= c
    qfeat = jnp.where(lax.broadcasted_iota(jnp.int32, (rows, hd), 1) < 2, 1.0, 0.0).astype(BF16)
    kfeat = kf_ref[0]

    for qi in range(nq):
        q = q_ref[0, qi * blk:(qi + 1) * blk, :]
        zero = jnp.zeros_like(q)
        qa = jnp.concatenate([jnp.concatenate([jnp.where(lane < hd // 2, q, zero),
                                               jnp.where(lane >= hd // 2, q, zero)], axis=0), qfeat], axis=1)
        m_sc[...] = jnp.full(m_sc.shape, NEG_BIG, F32)
        l_sc[...] = jnp.zeros(l_sc.shape, F32)
        acc_sc[...] = jnp.zeros(acc_sc.shape, F32)

        def block(start, width, masked, qa=qa):
            ka = jnp.concatenate([k_ref[0, pl.ds(start, width), :], kfeat[:width]], axis=1)
            s = _dot_nt(qa, ka)
            if masked:
                s = jnp.where(causal, s, NEG_BIG)
            off = slope * lax.convert_element_type(start, F32)
            chunks = [s[:, j * LANES:(j + 1) * LANES] for j in range(width // LANES)]
            mx = functools.reduce(jnp.maximum, chunks)
            m_prev = m_sc[...]
            m_new = jnp.maximum(m_prev, jnp.max(mx, axis=-1, keepdims=True) + off)
            shift = m_new - off
            ps = [jnp.exp(ch - shift) for ch in chunks]
            alpha = jnp.exp(m_prev - m_new)
            l_sc[...] = alpha * l_sc[...] + functools.reduce(jnp.add, ps)
            p = jnp.concatenate([x.astype(BF16) for x in ps], axis=1)
            acc_sc[...] = alpha * acc_sc[...] + _dot(p, v_ref[0, pl.ds(start, width), :])
            m_sc[...] = m_new

        if qi // 2 > 0:
            def body(i, carry):
                block(pl.multiple_of(i * 2 * blk, 2 * blk), 2 * blk, False)
                return carry
            lax.fori_loop(0, qi // 2, body, 0)
        if qi % 2 == 1:
            block((qi - 1) * blk, blk, False)
        block(qi * blk, blk, True)

        o = acc_sc[...] / jnp.sum(l_sc[...], axis=-1, keepdims=True)
        out = o[:blk] - lam * o[blk:]
        out = _rms(out, sw_ref[...], SUBLN_EPS) * (1.0 - lam_init)
        o_ref[0, qi * blk:(qi + 1) * blk, :] = out.astype(o_ref.dtype)


def attn_prompt(q, k, v, slopes, lams, subln_w, lam_init, name):
    bsz, s, w = q.shape
    heads = slopes.shape[0]
    hd = w // heads
    assert hd == LANES and 8 % heads == 0
    blk = min(256, s)
    wide = min(2 * blk, s)
    assert s % blk == 0 and blk % LANES == 0 and wide <= 2 * ALIBI_SPLIT * ALIBI_SPLIT
    kpos = jnp.arange(wide)
    feat = jnp.zeros((wide, hd), F32).at[:, 0].set((kpos // ALIBI_SPLIT * ALIBI_SPLIT).astype(F32))
    feat = feat.at[:, 1].set((kpos % ALIBI_SPLIT).astype(F32))
    kfeat = (slopes[:, None, None] * feat[None]).astype(BF16)

    lam_specs = [pl.BlockSpec(lams[0].shape, lambda b, h: (0, 0)) for _ in range(4)]
    return pl.pallas_call(
        functools.partial(_attn_prompt_kernel, blk=blk, lam_init=lam_init),
        out_shape=jax.ShapeDtypeStruct((bsz, s, w), BF16),
        grid=(bsz, heads),
        in_specs=[pl.BlockSpec(memory_space=pltpu.SMEM),
                  pl.BlockSpec((1, s, hd), lambda b, h: (b, 0, h)),
                  pl.BlockSpec((1, s, hd), lambda b, h: (b, 0, h)),
                  pl.BlockSpec((1, s, hd), lambda b, h: (b, 0, h)),
                  pl.BlockSpec((1, wide, hd), lambda b, h: (h, 0, 0))] + lam_specs
                 + [pl.BlockSpec((1, hd), lambda b, h: (0, 0))],
        out_specs=pl.BlockSpec((1, s, hd), lambda b, h: (b, 0, h)),
        scratch_shapes=[pltpu.VMEM((2 * blk, LANES), F32), pltpu.VMEM((2 * blk, LANES), F32),
                        pltpu.VMEM((2 * blk, hd), F32)],
        compiler_params=_cparams(("parallel", "parallel")),
        name=name,
    )(slopes, q, k, v, kfeat, *lams, subln_w.reshape(1, hd))


PAGES_PER_STEP = 8


def _attn_sample_kernel(pt_ref, q_ref, *refs, heads, steps, n_pages, pps, lam_init):
    ck_refs, cv_refs = refs[:pps], refs[pps:2 * pps]
    kn_ref, vn_ref, lq1, lk1, lq2, lk2, sw_ref, o_ref, m_sc, l_sc, acc_sc = refs[2 * pps:]
    p = pl.program_id(1)
    hd = sw_ref.shape[-1]
    hrows = 2 * steps
    nrow = heads * hrows
    past = n_pages * PAGE_SIZE
    n_steps = n_pages // pps

    @pl.when(p == 0)
    def _():
        m_sc[...] = jnp.full(m_sc.shape, NEG_BIG, F32)
        l_sc[...] = jnp.zeros(l_sc.shape, F32)
        acc_sc[...] = jnp.zeros(acc_sc.shape, F32)

    r = lax.broadcasted_iota(jnp.int32, (nrow, PAGE_SIZE), 0)
    col = lax.broadcasted_iota(jnp.int32, (nrow, PAGE_SIZE), 1)
    rt = r % steps
    slope = jnp.exp2(-8.0 * (r // hrows + 1).astype(F32) / heads)

    def head_rows(ref, hh):
        return ref[0, pl.ds(hh, PAGE_SIZE, stride=heads), :].astype(BF16)

    def scores(k_ref):
        return jnp.concatenate([_dot_nt(q_ref[0, hh], head_rows(k_ref, hh)) for hh in range(heads)],
                               axis=0)

    def update(s_pages, v_refs):
        s = s_pages[0] if len(s_pages) == 1 else jnp.concatenate(s_pages, axis=1)
        m_prev = m_sc[...]
        m_new = jnp.maximum(m_prev, jnp.max(s, axis=-1, keepdims=True))
        pe = jnp.exp(s - m_new)
        corr = jnp.exp(m_prev - m_new)
        l_sc[...] = l_sc[...] * corr + jnp.sum(pe, axis=-1, keepdims=True)
        acc = acc_sc[...] * corr
        for j, v_ref in enumerate(v_refs):
            pj = pe[:, j * PAGE_SIZE:(j + 1) * PAGE_SIZE].astype(BF16)
            acc = acc + jnp.concatenate(
                [_dot(pj[hh * hrows:(hh + 1) * hrows], head_rows(v_ref, hh)) for hh in range(heads)],
                axis=0)
        acc_sc[...] = acc
        m_sc[...] = m_new

    @pl.when(p < n_steps)
    def _():
        s_pages = []
        for j in range(pps):
            dist = past + rt - ((p * pps + j) * PAGE_SIZE + col)
            s_pages.append(scores(ck_refs[j]) - slope * dist.astype(F32))
        update(s_pages, cv_refs)

    @pl.when(p == n_steps)
    def _():
        dist = rt - col
        s = jnp.where((dist >= 0) & (col < steps), scores(kn_ref) - slope * dist.astype(F32), NEG_BIG)
        update([s], [vn_ref])
        od = acc_sc[...] / l_sc[...]
        lam = _lambda_value(lq1, lk1, lq2, lk2, lam_init)
        out = od - lam * pltpu.roll(od, nrow - steps, 0)
        out = _rms(out, sw_ref[...], SUBLN_EPS) * (1.0 - lam_init)
        o_ref[0] = out.astype(o_ref.dtype)


def attn_sample(qh, cache_k, cache_v, knew, vnew, page_table, lams, subln_w, lam_init, steps, name):
    db, heads, hrows, hd = qh.shape
    n_pages = page_table.shape[1]
    pps = math.gcd(PAGES_PER_STEP, n_pages)
    n_steps = n_pages // pps
    pt = page_table.reshape(-1).astype(jnp.int32)
    cache_k, cache_v, knew, vnew = [a.reshape(a.shape[0], PAGE_SIZE * heads, hd) for a in (cache_k, cache_v, knew, vnew)]

    def page_map(j):
        return lambda b, p, pt: (pt[b * n_pages + jnp.minimum(p, n_steps - 1) * pps + j], 0, 0)

    page_specs = [pl.BlockSpec((1, PAGE_SIZE * heads, hd), page_map(j)) for j in range(pps)]
    new_spec = pl.BlockSpec((1, PAGE_SIZE * heads, hd), lambda b, p, pt: (b, 0, 0))
    lam_specs = [pl.BlockSpec(lams[0].shape, lambda b, p, pt: (0, 0)) for _ in range(4)]
    grid_spec = pltpu.PrefetchScalarGridSpec(
        num_scalar_prefetch=1,
        grid=(db, n_steps + 1),
        in_specs=[pl.BlockSpec((1, heads, hrows, hd), lambda b, p, pt: (b, 0, 0, 0))] + page_specs + page_specs
                 + [new_spec, new_spec] + lam_specs + [pl.BlockSpec((1, hd), lambda b, p, pt: (0, 0))],
        out_specs=pl.BlockSpec((1, heads * hrows, hd), lambda b, p, pt: (b, 0, 0)),
        scratch_shapes=[pltpu.VMEM((heads * hrows, 1), F32), pltpu.VMEM((heads * hrows, 1), F32),
                        pltpu.VMEM((heads * hrows, hd), F32)],
    )
    return pl.pallas_call(
        functools.partial(_attn_sample_kernel, heads=heads, steps=steps, n_pages=n_pages, pps=pps,
                          lam_init=lam_init),
        out_shape=jax.ShapeDtypeStruct((db, heads * hrows, hd), F32),
        grid_spec=grid_spec,
        compiler_params=_cparams(("parallel", "arbitrary")),
        name=name,
    )(pt, qh, *([cache_k] * pps), *([cache_v] * pps), knew, vnew, *lams, subln_w.reshape(1, hd))


def _router_weights(w_group, w_sub):
    d = w_group.shape[0]
    ws = jnp.transpose(w_sub, (1, 0, 2)).reshape(d, MOE_GROUPS * MOE_EPG)
    pad = jnp.zeros((d, LANES - MOE_GROUPS - MOE_GROUPS * MOE_EPG), F32)
    return jnp.concatenate([w_group.astype(F32), ws.astype(F32), pad], axis=1)


def _head_queries(q, heads, steps):
    db = q.shape[0]
    dh = q.shape[-1] // heads // 2
    q5 = q.reshape(db, steps, heads, 2, dh)
    eye_c = jnp.eye(2, dtype=q.dtype)
    qh = jnp.einsum('bthcd,ce->bhcted', q5, eye_c)
    return qh.reshape(db, heads, 2 * steps, 2 * dh)


def _trunk(x, s0_all, paged, wts, tag):
    bsz, s, d = x.shape
    t = bsz * s
    depth = wts['norm_mix_w'].shape[0]
    n_a = wts['a_w_in'].shape[0]
    heads_a = s0_all.shape[2]
    heads_b = wts['slopes'].shape[0]
    hd = d // heads_b
    h = x.reshape(t, d)
    states = []
    k_f = v_f = k_b = v_b = None
    precise = paged is not None
    mm = wts['f32'] if precise else wts['bf16']
    for l in range(depth):
        if l < n_a:
            (proj,) = norm_matmul(h, wts['norm_mix_w'][l], mm['a_w_in'][l], [(0, 4 * d, 1.0, 1)], [F32],
                                  f"{tag}_l{l}_inproj")
            proj = proj.reshape(bsz, s, 4 * d)
            if s >= GLA_CHUNK:
                o, s_new = gla(proj, wts['a_lb_logits'], wts['a_gnorm_w'][l], s0_all[l], l, f"{tag}_l{l}_gla")
            else:
                rows = -(-s // SUBLANES) * SUBLANES
                proj = jnp.pad(proj, ((0, 0), (0, rows - s), (0, 0)))
                o, s_new = gla_steps(proj, wts['a_lb_logits'], wts['a_gnorm_w'][l], s0_all[l], l, s,
                                     f"{tag}_l{l}_gla")
                o = o[:, :s]
            states.append(s_new)
            h = matmul_residual(o.reshape(t, d), mm['a_w_o'][l], h, f"{tag}_l{l}_oproj")
        else:
            j = l - n_a
            lam_init = 0.8 - 0.6 * math.exp(-0.3 * l)
            lams = [wts[nm][j].reshape(1, -1) for nm in ('b_lambda_q1', 'b_lambda_k1', 'b_lambda_q2', 'b_lambda_k2')]
            (q,) = norm_matmul(h, wts['norm_mix_w'][l], mm['b_w_q'][j], [(0, d, (hd // 2) ** -0.5, 1)],
                               [F32 if precise else BF16], f"{tag}_l{l}_qproj")
            if paged is None:
                o = attn_prompt(q.reshape(bsz, s, d), k_b.reshape(bsz, s, d), v_b.reshape(bsz, s, d),
                                wts['slopes'], lams, wts['b_subln_w'][j], lam_init, f"{tag}_l{l}_attn")
                o = o.reshape(t, d)
            else:
                cache_k, cache_v, page_table = paged
                qh = _head_queries(q.reshape(bsz, s, d), heads_b, s).astype(BF16)
                padn = ((0, 0), (0, PAGE_SIZE - s), (0, 0), (0, 0))
                o = attn_sample(qh, cache_k, cache_v, jnp.pad(k_f.reshape(bsz, s, heads_b, hd), padn),
                                jnp.pad(v_f.reshape(bsz, s, heads_b, hd), padn), page_table, lams,
                                wts['b_subln_w'][j], lam_init, s, f"{tag}_l{l}_attn")
                o = o.reshape(bsz, heads_b, 2, s, hd)[:, :, 0].transpose(0, 2, 1, 3).reshape(t, d)
            h = matmul_residual(o, mm['b_w_o'][j], h, f"{tag}_l{l}_oproj")
        h = moe_layer(h, wts['norm_ffn_w'][l], wts['router_w_f32' if precise else 'router_w_split'][l],
                      mm['moe_w_gate'][l], mm['moe_w_up'][l], mm['moe_w_down'][l], f"{tag}_l{l}_moe")
        if l == n_a - 1:
            k_f, k_b, v_f, v_b = norm_matmul(h, wts['kv_norm_w'], mm['w_kv'], [(0, d, 1.0, 2), (d, d, 1.0, 2)],
                                            [F32, BF16, F32, BF16], f"{tag}_kvproj")
    y = rmsnorm_rows(h, wts['final_norm_w'], f"{tag}_final_norm")
    return (y.reshape(bsz, s, d), jnp.stack(states),
            k_f.reshape(bsz, s, heads_b, hd), v_f.reshape(bsz, s, heads_b, hd))


def kernel(x_prompt, x_sample, state_hgrn, cache_k, cache_v, page_table, norm_mix_w, norm_ffn_w, a_w_in,
           a_lb_logits, a_gnorm_w, a_w_o, kv_norm_w, w_kv, b_w_q, b_lambda_q1, b_lambda_k1, b_lambda_q2,
           b_lambda_k2, b_subln_w, b_w_o, moe_w_group, moe_w_sub, moe_w_gate, moe_w_up, moe_w_down,
           final_norm_w):
    depth = norm_mix_w.shape[0]
    heads_b = cache_k.shape[2]
    mats = dict(a_w_in=a_w_in, a_w_o=a_w_o, w_kv=w_kv, b_w_q=b_w_q, b_w_o=b_w_o,
                moe_w_gate=moe_w_gate, moe_w_up=moe_w_up, moe_w_down=moe_w_down)
    router_w = jnp.stack([_router_weights(moe_w_group[l], moe_w_sub[l]) for l in range(depth)])
    router_hi = router_w.astype(BF16)
    router_lo = (router_w - router_hi.astype(F32)).astype(BF16)
    wts = dict(
        norm_mix_w=norm_mix_w, norm_ffn_w=norm_ffn_w, a_lb_logits=a_lb_logits, a_w_in=a_w_in,
        a_gnorm_w=a_gnorm_w, kv_norm_w=kv_norm_w, b_lambda_q1=b_lambda_q1, b_lambda_k1=b_lambda_k1,
        b_lambda_q2=b_lambda_q2, b_lambda_k2=b_lambda_k2, b_subln_w=b_subln_w,
        router_w_f32=router_w, router_w_split=jnp.stack([router_hi, router_lo], axis=1),
        final_norm_w=final_norm_w,
        slopes=2.0 ** (-8.0 * jnp.arange(1, heads_b + 1, dtype=F32) / heads_b),
        f32=mats, bf16={k: v.astype(BF16) for k, v in mats.items()},
    )
    n_a = a_w_in.shape[0]
    bsz = x_prompt.shape[0]
    s0_prompt = jnp.zeros((n_a, bsz) + state_hgrn.shape[2:], F32)
    y_p, st_p, k_p, v_p = _trunk(x_prompt, s0_prompt, None, wts, "p")
    paged = (cache_k, cache_v, page_table)
    y_s, st_s, k_s, v_s = _trunk(x_sample, state_hgrn, paged, wts, "s")
    return (y_p, y_s, st_p, st_s, k_p, v_p, k_s, v_s)
```

```python
import functools
import math

import jax
import jax.numpy as jnp
from jax import lax
from jax.experimental import pallas as pl
from jax.experimental.pallas import tpu as pltpu

F32 = jnp.float32
BF16 = jnp.bfloat16
HIGHEST = lax.Precision.HIGHEST

LANES = 128
SUBLANES = 8
VMEM_LIMIT_BYTES = 48 * 1024 * 1024

NORM_EPS = 1e-6
SUBLN_EPS = 1e-5
NEG_BIG = -1e30
GLA_CHUNK = 128
MOE_GROUPS = 4
MOE_EPG = 4
PAGE_SIZE = 128


def _cparams(sem):
    return pltpu.CompilerParams(dimension_semantics=sem, vmem_limit_bytes=VMEM_LIMIT_BYTES)


def _rms(x, w, eps):
    var = jnp.mean(x * x, axis=-1, keepdims=True)
    return x * lax.rsqrt(var + eps) * w


def _sigmoid(x):
    return 1.0 / (1.0 + jnp.exp(-x))


def _dot(a, b):
    return jnp.dot(a, b, preferred_element_type=F32)


def _dot_f32(a, b):
    return jnp.dot(a, b, precision=HIGHEST, preferred_element_type=F32)


def _dot_nt(a, b):
    return lax.dot_general(a, b, (((1,), (1,)), ((), ())), preferred_element_type=F32)


def _dot_tn(a, b):
    return lax.dot_general(a, b, (((0,), (0,)), ((), ())), preferred_element_type=F32)


def _row_tile(t, pref):
    tm = min(pref, t)
    assert t % tm == 0, (t, tm)
    return tm


def _norm_matmul_kernel(x_ref, nw_ref, w_ref, *out_refs, groups, nchunk):
    hn = _rms(x_ref[...], nw_ref[...], NORM_EPS)
    precise = w_ref.dtype == F32
    if not precise:
        hn = hn.astype(BF16)
    k = 0
    for (c0, width, scale, n_out) in groups:
        refs = out_refs[k:k + n_out]
        k += n_out
        for j in range(0, width, nchunk):
            wblk = w_ref[:, c0 + j:c0 + j + nchunk]
            acc = _dot_f32(hn, wblk) if precise else _dot(hn, wblk)
            if scale != 1.0:
                acc = acc * scale
            for r in refs:
                r[:, j:j + nchunk] = acc.astype(r.dtype)


def norm_matmul(x, nw, w, groups, out_dtypes, name):
    t, d = x.shape
    tm = _row_tile(t, 256)
    widths = []
    for (c0, width, scale, n_out) in groups:
        widths += [width] * n_out
    nchunk = min(512, min(widths))
    out_shape = [jax.ShapeDtypeStruct((t, wd), dt) for wd, dt in zip(widths, out_dtypes)]
    return pl.pallas_call(
        functools.partial(_norm_matmul_kernel, groups=tuple(groups), nchunk=nchunk),
        out_shape=out_shape,
        grid=(t // tm,),
        in_specs=[pl.BlockSpec((tm, d), lambda i: (i, 0)),
                  pl.BlockSpec((1, d), lambda i: (0, 0)),
                  pl.BlockSpec(w.shape, lambda i: (0, 0))],
        out_specs=[pl.BlockSpec((tm, wd), lambda i: (i, 0)) for wd in widths],
        compiler_params=_cparams(("parallel",)),
        name=name,
    )(x, nw.reshape(1, d), w)


def _matmul_res_kernel(x_ref, w_ref, r_ref, o_ref):
    if w_ref.dtype == F32:
        o_ref[...] = r_ref[...] + _dot_f32(x_ref[...], w_ref[...])
    else:
        o_ref[...] = r_ref[...] + _dot(x_ref[...], w_ref[...])


def matmul_residual(x, w, res, name):
    t, kdim = x.shape
    n = w.shape[1]
    tm = _row_tile(t, 512)
    return pl.pallas_call(
        _matmul_res_kernel,
        out_shape=jax.ShapeDtypeStruct((t, n), F32),
        grid=(t // tm,),
        in_specs=[pl.BlockSpec((tm, kdim), lambda i: (i, 0)),
                  pl.BlockSpec((kdim, n), lambda i: (0, 0)),
                  pl.BlockSpec((tm, n), lambda i: (i, 0))],
        out_specs=pl.BlockSpec((tm, n), lambda i: (i, 0)),
        compiler_params=_cparams(("parallel",)),
        name=name,
    )(x, w, res)


def _rmsnorm_kernel(x_ref, w_ref, o_ref):
    o_ref[...] = _rms(x_ref[...], w_ref[...], NORM_EPS)


def rmsnorm_rows(x, w, name):
    t, d = x.shape
    tm = _row_tile(t, 512)
    return pl.pallas_call(
        _rmsnorm_kernel,
        out_shape=jax.ShapeDtypeStruct((t, d), F32),
        grid=(t // tm,),
        in_specs=[pl.BlockSpec((tm, d), lambda i: (i, 0)), pl.BlockSpec((1, d), lambda i: (0, 0))],
        out_specs=pl.BlockSpec((tm, d), lambda i: (i, 0)),
        compiler_params=_cparams(("parallel",)),
        name=name,
    )(x, w.reshape(1, d))


def _forget_lower_bound(logits, layer):
    le = jnp.exp(logits - jnp.max(logits, axis=0, keepdims=True))
    lp = le / jnp.sum(le, axis=0, keepdims=True)
    lb = jnp.zeros((1, logits.shape[1]), F32)
    for li in range(1, layer + 1):
        lb = lb + lp[li:li + 1, :]
    return lb


def _gla_kernel(q_ref, f_ref, i_ref, g_ref, lbl_ref, gw_ref, s0_ref, o_ref, sn_ref, st_ref,
                *, chunk, layer, heads, dk):
    n = pl.program_id(1)
    c = chunk

    @pl.when(n == 0)
    def _():
        for h in range(heads):
            st_ref[h] = s0_ref[0, h].T

    lb_all = _forget_lower_bound(lbl_ref[...], layer)

    row = lax.broadcasted_iota(jnp.int32, (c, dk), 0)
    rr = lax.broadcasted_iota(jnp.int32, (c, c), 0)
    cc = lax.broadcasted_iota(jnp.int32, (c, c), 1)
    tri = (rr >= cc).astype(F32)
    ones_b = jnp.ones((dk, c), BF16)
    gw = gw_ref[...]
    nblk = c // SUBLANES
    srow = lax.broadcasted_iota(jnp.int32, (SUBLANES, dk), 0)
    lane_c = lax.broadcasted_iota(jnp.int32, (SUBLANES, c), 1)

    for h in range(heads):
        sl = slice(h * dk, (h + 1) * dk)
        lb = lb_all[:, sl]
        fg = lb + (1.0 - lb) * _sigmoid(f_ref[0, :, sl])
        logf = jnp.log(fg)
        kk = 1.0 - fg
        qf = q_ref[0, :, sl]
        qs = qf * _sigmoid(qf)
        v = i_ref[0, :, sl]
        vb = v.astype(BF16)
        b = lax.dot_general(tri, logf, (((1,), (0,)), ((), ())), precision=HIGHEST,
                            preferred_element_type=F32)
        st = st_ref[h]
        inter = _dot_nt((qs * jnp.exp(b)).astype(BF16), st.astype(BF16))

        att = jnp.zeros((c, c), F32)
        half = c // 2
        while half >= SUBLANES:
            lower = ((row // half) % 2) == 1
            mids = []
            for blk in range(c // (2 * half)):
                mid = b[blk * 2 * half + half - 1: blk * 2 * half + half, :]
                mids.append(jnp.broadcast_to(mid, (2 * half, dk)))
            bmid = mids[0] if len(mids) == 1 else jnp.concatenate(mids, axis=0)
            ex = jnp.exp(jnp.where(lower, b - bmid, bmid - b))
            qd = jnp.where(lower, qs * ex, 0.0).astype(BF16)
            kd = jnp.where(lower, 0.0, kk * ex).astype(BF16)
            a = _dot_nt(qd, kd)
            same = (rr // (2 * half)) == (cc // (2 * half))
            att = att + jnp.where(same, a, 0.0)
            half //= 2

        pieces = []
        for blk in range(nblk):
            r0 = blk * SUBLANES
            bb = b[r0:r0 + SUBLANES, :]
            qb = qs[r0:r0 + SUBLANES, :]
            kb = kk[r0:r0 + SUBLANES, :]
            for s in range(SUBLANES):
                keep = srow >= s
                dec = jnp.exp(jnp.where(keep, bb - bb[s:s + 1, :], 0.0))
                pieces.append(jnp.where(keep, qb * kb[s:s + 1, :] * dec, 0.0).astype(BF16))
        pst = jnp.concatenate(pieces, axis=0)
        red = _dot(pst, ones_b)
        dblocks = []
        for blk in range(nblk):
            acc = jnp.zeros((SUBLANES, c), F32)
            for s in range(SUBLANES):
                idx = blk * SUBLANES + s
                acc = jnp.where(lane_c == idx, red[idx * SUBLANES:(idx + 1) * SUBLANES, :], acc)
            dblocks.append(acc)
        att = att + jnp.concatenate(dblocks, axis=0)

        out = inter + _dot(att.astype(BF16), vb)
        out = _rms(out, gw, NORM_EPS)
        gf = g_ref[0, :, sl]
        o_ref[0, :, sl] = (out * (gf * _sigmoid(gf))).astype(o_ref.dtype)

        bl = b[c - 1:c, :]
        kdl = (kk * jnp.exp(bl - b)).astype(BF16)
        st_new = st * jnp.exp(bl) + _dot_tn(vb, kdl)
        st_ref[h] = st_new

    @pl.when(n == pl.num_programs(1) - 1)
    def _():
        for h in range(heads):
            sn_ref[0, h] = st_ref[h].T


def gla(proj, lb_logits, gnorm_w, s0, layer, name):
    bsz, s, w4 = proj.shape
    w = w4 // 4
    _, heads, dk, dv = s0.shape
    c = min(GLA_CHUNK, s)
    assert s % c == 0 and c % (2 * SUBLANES) == 0 and dk == dv == LANES
    nl = lb_logits.shape[0]
    kern = functools.partial(_gla_kernel, chunk=c, layer=layer, heads=heads, dk=dk)
    return pl.pallas_call(
        kern,
        out_shape=[jax.ShapeDtypeStruct((bsz, s, w), BF16), jax.ShapeDtypeStruct(s0.shape, F32)],
        grid=(bsz, s // c),
        in_specs=[pl.BlockSpec((1, c, w), lambda b, n: (b, n, 0)),
                  pl.BlockSpec((1, c, w), lambda b, n: (b, n, 1)),
                  pl.BlockSpec((1, c, w), lambda b, n: (b, n, 2)),
                  pl.BlockSpec((1, c, w), lambda b, n: (b, n, 3)),
                  pl.BlockSpec((nl, w), lambda b, n: (0, 0)),
                  pl.BlockSpec((1, dv), lambda b, n: (0, 0)),
                  pl.BlockSpec((1, heads, dk, dv), lambda b, n: (b, 0, 0, 0))],
        out_specs=[pl.BlockSpec((1, c, w), lambda b, n: (b, n, 0)),
                   pl.BlockSpec((1, heads, dk, dv), lambda b, n: (b, 0, 0, 0))],
        scratch_shapes=[pltpu.VMEM((heads, dv, dk), F32)],
        compiler_params=_cparams(("parallel", "arbitrary")),
        name=name,
    )(proj, proj, proj, proj, lb_logits, gnorm_w.reshape(1, dv), s0)


def _gla_steps_kernel(p_ref, lbl_ref, gw_ref, s0_ref, o_ref, sn_ref, *, steps, layer, heads, dk):
    w = heads * dk
    rows = p_ref.shape[1]
    lb_all = _forget_lower_bound(lbl_ref[...], layer)
    gw = gw_ref[...]
    for h in range(heads):
        qf = p_ref[0, :, h * dk:(h + 1) * dk]
        ff = p_ref[0, :, w + h * dk:w + (h + 1) * dk]
        v = p_ref[0, :, 2 * w + h * dk:2 * w + (h + 1) * dk]
        gf = p_ref[0, :, 3 * w + h * dk:3 * w + (h + 1) * dk]
        lb = lb_all[:, h * dk:(h + 1) * dk]
        fg = lb + (1.0 - lb) * _sigmoid(ff)
        qs = qf * _sigmoid(qf)
        stack = jnp.concatenate([qs, fg, 1.0 - fg, jnp.zeros((dk - 3 * rows, dk), F32)], axis=0)
        cols = stack.T
        st = s0_ref[0, h]
        outs = []
        for t in range(steps):
            st = st * cols[:, rows + t:rows + t + 1] + cols[:, 2 * rows + t:2 * rows + t + 1] * v[t:t + 1, :]
            outs.append(jnp.sum(st * cols[:, t:t + 1], axis=0, keepdims=True))
        sn_ref[0, h] = st
        out = jnp.concatenate(outs + [jnp.zeros((rows - steps, dk), F32)], axis=0)
        out = _rms(out, gw, NORM_EPS)
        o_ref[0, :, h * dk:(h + 1) * dk] = out * (gf * _sigmoid(gf))


def gla_steps(proj, lb_logits, gnorm_w, s0, layer, steps, name):
    bsz, rows, w4 = proj.shape
    w = w4 // 4
    _, heads, dk, dv = s0.shape
    assert rows % SUBLANES == 0 and 3 * rows <= dk and dk == dv == LANES
    nl = lb_logits.shape[0]
    return pl.pallas_call(
        functools.partial(_gla_steps_kernel, steps=steps, layer=layer, heads=heads, dk=dk),
        out_shape=[jax.ShapeDtypeStruct((bsz, rows, w), F32), jax.ShapeDtypeStruct(s0.shape, F32)],
        grid=(bsz,),
        in_specs=[pl.BlockSpec((1, rows, w4), lambda b: (b, 0, 0)),
                  pl.BlockSpec((nl, w), lambda b: (0, 0)),
                  pl.BlockSpec((1, dv), lambda b: (0, 0)),
                  pl.BlockSpec((1, heads, dk, dv), lambda b: (b, 0, 0, 0))],
        out_specs=[pl.BlockSpec((1, rows, w), lambda b: (b, 0, 0)),
                   pl.BlockSpec((1, heads, dk, dv), lambda b: (b, 0, 0, 0))],
        compiler_params=_cparams(("parallel",)),
        name=name,
    )(proj, lb_logits, gnorm_w.reshape(1, dv), s0)


def _router_logits(hn, wr_ref):
    if wr_ref.dtype == F32:
        return _dot_f32(hn, wr_ref[...])
    hi = hn.astype(BF16)
    lo = (hn - hi.astype(F32)).astype(BF16)
    return _dot(hi, wr_ref[0]) + (_dot(lo, wr_ref[0]) + _dot(hi, wr_ref[1]))


def _router_kernel(h_ref, nw_ref, wr_ref, gid_ref):
    hn = _rms(h_ref[...], nw_ref[...], NORM_EPS)
    logits = _router_logits(hn, wr_ref)
    lane = lax.broadcasted_iota(jnp.int32, logits.shape, 1)
    isg = lane < MOE_GROUPS
    pgl = jnp.where(isg, logits, -jnp.inf)
    m = jnp.max(pgl, axis=-1, keepdims=True)
    gidx = jnp.min(jnp.where(isg & (pgl == m), lane, LANES), axis=-1, keepdims=True)
    rep = jnp.broadcast_to(gidx.astype(F32), logits.shape).astype(BF16)
    avg = jnp.full((SUBLANES, LANES), 1.0 / LANES, BF16)
    gid_ref[...] = _dot_nt(avg, rep).astype(jnp.int32)


def moe_router(h, nw, wr, name):
    t, d = h.shape
    tm = _row_tile(t, 512)
    gid = pl.pallas_call(
        _router_kernel,
        out_shape=jax.ShapeDtypeStruct((t // tm * SUBLANES, tm), jnp.int32),
        grid=(t // tm,),
        in_specs=[pl.BlockSpec((tm, d), lambda i: (i, 0)),
                  pl.BlockSpec((1, d), lambda i: (0, 0)),
                  pl.BlockSpec(wr.shape, lambda i: (0,) * wr.ndim)],
        out_specs=pl.BlockSpec((SUBLANES, tm), lambda i: (i, 0)),
        compiler_params=_cparams(("parallel",)),
        name=name,
    )(h, nw.reshape(1, d), wr)
    return gid.reshape(t // tm, SUBLANES, tm)[:, 0, :].reshape(t)


DMA_UNROLL = 8


def _moe_ffn_kernel(tg_ref, nv_ref, ridx_ref, h_hbm, nw_ref, wr_ref, wg_ref, wu_ref, wd_ref, out_hbm,
                    xbuf, obuf, gsem, ssem, *, tm):
    t = pl.program_id(0)
    n_tiles = pl.num_programs(0)
    nv = nv_ref[t]
    g = tg_ref[t]
    slot = t % 2

    def gather_rows(tile, sl):
        def body(i, carry):
            for u in range(DMA_UNROLL):
                r = i * DMA_UNROLL + u
                pltpu.make_async_copy(h_hbm.at[pl.ds(ridx_ref[tile * tm + r], 1)],
                                      xbuf.at[sl, pl.ds(r, 1)], gsem.at[sl]).start()
            return carry
        lax.fori_loop(0, tm // DMA_UNROLL, body, 0)

    def scatter_row(r):
        pltpu.make_async_copy(obuf.at[pl.ds(r, 1)], out_hbm.at[pl.ds(ridx_ref[t * tm + r], 1)], ssem).start()

    def wait_scatter(count):
        groups = count // SUBLANES

        @pl.when(groups > 0)
        def _():
            n = pl.multiple_of(groups * SUBLANES, SUBLANES)
            pltpu.make_async_copy(obuf.at[pl.ds(0, n)], out_hbm.at[pl.ds(0, n)], ssem).wait()

        def one(r, carry):
            pltpu.make_async_copy(obuf.at[pl.ds(0, 1)], out_hbm.at[pl.ds(0, 1)], ssem).wait()
            return carry
        lax.fori_loop(groups * SUBLANES, count, one, 0)

    def wait_gather(sl):
        pltpu.make_async_copy(h_hbm.at[pl.ds(0, tm)], xbuf.at[sl], gsem.at[sl]).wait()

    @pl.when(t == 0)
    def _():
        gather_rows(t, slot)

    nxt = jnp.minimum(t + 1, n_tiles - 1)
    has_next = (t + 1 < n_tiles) & (nv_ref[nxt] > 0)

    @pl.when(nv > 0)
    def _():
        wait_gather(slot)
        x = xbuf[slot]
        for r in range(tm):
            pltpu.make_async_copy(h_hbm.at[pl.ds(ridx_ref[nxt * tm + r], 1)],
                                  xbuf.at[1 - slot, pl.ds(r, 1)], gsem.at[1 - slot]).start()
        hn = _rms(x, nw_ref[...], NORM_EPS)
        logits = _router_logits(hn, wr_ref)
        lane = lax.broadcasted_iota(jnp.int32, logits.shape, 1)
        isg = lane < MOE_GROUPS
        m = jnp.max(jnp.where(isg, logits, -jnp.inf), axis=-1, keepdims=True)
        e = jnp.where(isg, jnp.exp(jnp.where(isg, logits - m, 0.0)), 0.0)
        gprob = jnp.sum(jnp.where(lane == g, e, 0.0), axis=-1, keepdims=True) / jnp.sum(e, axis=-1, keepdims=True)
        lo = MOE_GROUPS + MOE_EPG * g
        inb = (lane >= lo) & (lane < lo + MOE_EPG)
        sv = jnp.where(inb, logits, -jnp.inf)
        m1 = jnp.max(sv, axis=-1, keepdims=True)
        i1 = jnp.min(jnp.where(inb & (sv == m1), lane, LANES), axis=-1, keepdims=True)
        inb2 = inb & (lane != i1)
        sv2 = jnp.where(inb2, logits, -jnp.inf)
        m2 = jnp.max(sv2, axis=-1, keepdims=True)
        i2 = jnp.min(jnp.where(inb2 & (sv2 == m2), lane, LANES), axis=-1, keepdims=True)
        e2 = jnp.exp(m2 - m1)
        w1 = gprob / (1.0 + e2)
        w2 = gprob * e2 / (1.0 + e2)
        gate = jnp.where(lane == i1, w1, jnp.where(lane == i2, w2, 0.0))

        precise = wg_ref.dtype == F32
        hb = hn if precise else hn.astype(BF16)
        dot = _dot_f32 if precise else _dot
        y = jnp.zeros(x.shape, F32)
        for ex in range(MOE_EPG):
            ge = jnp.sum(jnp.where(lane == lo + ex, gate, 0.0), axis=-1, keepdims=True)
            a = dot(hb, wg_ref[ex])
            u = dot(hb, wu_ref[ex])
            hid = a * _sigmoid(a) * u * ge
            y = y + dot(hid if precise else hid.astype(BF16), wd_ref[ex])
        @pl.when(t > 0)
        def _():
            wait_scatter(nv_ref[jnp.maximum(t - 1, 0)])

        obuf[...] = x + y

        nfull = nv // DMA_UNROLL

        def issue_out(i, carry):
            for u in range(DMA_UNROLL):
                scatter_row(i * DMA_UNROLL + u)
            return carry

        def issue_out_tail(r, carry):
            scatter_row(r)
            return carry

        @pl.when(nv == tm)
        def _():
            for r in range(tm):
                scatter_row(r)

        @pl.when(nv < tm)
        def _():
            lax.fori_loop(0, nfull, issue_out, 0)
            lax.fori_loop(nfull * DMA_UNROLL, nv, issue_out_tail, 0)

        @pl.when(jnp.logical_not(has_next))
        def _():
            wait_scatter(nv)
            wait_gather(1 - slot)


def _moe_plan(gidx, t, tm):
    ng = MOE_GROUPS
    n_tiles = t // tm + ng
    counts = jnp.sum((gidx[:, None] == jnp.arange(ng)[None, :]).astype(jnp.int32), axis=0)
    tiles_g = (counts + tm - 1) // tm
    tiles_end = jnp.cumsum(tiles_g)
    first_tile = tiles_end - tiles_g
    start_g = jnp.cumsum(counts) - counts
    used = tiles_end[-1]
    perm = jnp.argsort(gidx, stable=True).astype(jnp.int32)
    tile_ids = jnp.arange(n_tiles, dtype=jnp.int32)
    tg = jnp.sum((tile_ids[:, None] >= tiles_end[None, :]).astype(jnp.int32), axis=1)
    last_g = jnp.sum((used - 1 >= tiles_end).astype(jnp.int32))
    tg = jnp.where(tile_ids < used, jnp.minimum(tg, ng - 1), last_g).astype(jnp.int32)
    j = tile_ids - first_tile[tg]
    nv = jnp.clip(counts[tg] - j * tm, 0, tm)
    nv = jnp.where(tile_ids < used, nv, 0).astype(jnp.int32)
    src0 = start_g[tg] + j * tm
    src = src0[:, None] + jnp.arange(tm, dtype=jnp.int32)[None, :]
    ok = jnp.arange(tm, dtype=jnp.int32)[None, :] < nv[:, None]
    src = jnp.where(ok, src, src0[:, None])
    ridx = perm[jnp.clip(src, 0, t - 1)].reshape(-1).astype(jnp.int32)
    return tg, nv, ridx


def moe_layer(h, nw, wr, wg, wu, wd, name):
    t, d = h.shape
    tm = _row_tile(t, 256)
    assert tm % DMA_UNROLL == 0
    gid = moe_router(h, nw, wr, name + "_router")
    tg, nv, ridx = _moe_plan(gid, t, tm)
    n_tiles = t // tm + MOE_GROUPS
    fdim = wg.shape[2]
    grid_spec = pltpu.PrefetchScalarGridSpec(
        num_scalar_prefetch=3,
        grid=(n_tiles,),
        in_specs=[pl.BlockSpec(memory_space=pl.ANY),
                  pl.BlockSpec((1, d), lambda i, tg, nv, ri: (0, 0)),
                  pl.BlockSpec(wr.shape, lambda i, tg, nv, ri: (0,) * wr.ndim),
                  pl.BlockSpec((MOE_EPG, d, fdim), lambda i, tg, nv, ri: (tg[i], 0, 0)),
                  pl.BlockSpec((MOE_EPG, d, fdim), lambda i, tg, nv, ri: (tg[i], 0, 0)),
                  pl.BlockSpec((MOE_EPG, fdim, d), lambda i, tg, nv, ri: (tg[i], 0, 0))],
        out_specs=pl.BlockSpec(memory_space=pl.ANY),
        scratch_shapes=[pltpu.VMEM((2, tm, d), F32), pltpu.VMEM((tm, d), F32),
                        pltpu.SemaphoreType.DMA((2,)), pltpu.SemaphoreType.DMA],
    )
    return pl.pallas_call(
        functools.partial(_moe_ffn_kernel, tm=tm),
        out_shape=jax.ShapeDtypeStruct((t, d), F32),
        grid_spec=grid_spec,
        compiler_params=_cparams(("arbitrary",)),
        name=name + "_ffn",
    )(tg, nv, ridx, h, nw.reshape(1, d), wr, wg, wu, wd)


def _lambda_value(lq1, lk1, lq2, lk2, lam_init):
    s1 = jnp.sum(lq1[...] * lk1[...], axis=-1, keepdims=True)
    s2 = jnp.sum(lq2[...] * lk2[...], axis=-1, keepdims=True)
    return jnp.exp(s1) - jnp.exp(s2) + lam_init


ALIBI_SPLIT = 16


def _attn_prompt_kernel(slopes_ref, q_ref, k_ref, v_ref, kf_ref, lq1, lk1, lq2, lk2, sw_ref, o_ref,
                        m_sc, l_sc, acc_sc, *, blk, lam_init):
    h = pl.program_id(1)
    s_len, hd = q_ref.shape[1], q_ref.shape[2]
    nq = s_len // blk
    rows = 2 * blk
    slope = slopes_ref[h]
    lam = _lambda_value(lq1, lk1, lq2, lk2, lam_init)
    lane = lax.broadcasted_iota(jnp.int32, (blk, hd), 1)
    r = lax.broadcasted_iota(jnp.int32, (rows, blk), 0)
    c = lax.broadcasted_iota(jnp.int32, (rows, blk), 1)
    causal = jnp.where(r >= blk, r - blk, r) >= c
    qfeat = jnp.where(lax.broadcasted_iota(jnp.int32, (rows, hd), 1) < 2, 1.0, 0.0).astype(BF16)
    kfeat = kf_ref[0]

    for qi in range(nq):
        q = q_ref[0, qi * blk:(qi + 1) * blk, :]
        zero = jnp.zeros_like(q)
        qa = jnp.concatenate([jnp.concatenate([jnp.where(lane < hd // 2, q, zero),
                                               jnp.where(lane >= hd // 2, q, zero)], axis=0), qfeat], axis=1)
        m_sc[...] = jnp.full(m_sc.shape, NEG_BIG, F32)
        l_sc[...] = jnp.zeros(l_sc.shape, F32)
        acc_sc[...] = jnp.zeros(acc_sc.shape, F32)

        def block(start, width, masked, qa=qa):
            ka = jnp.concatenate([k_ref[0, pl.ds(start, width), :], kfeat[:width]], axis=1)
            s = _dot_nt(qa, ka)
            if masked:
                s = jnp.where(causal, s, NEG_BIG)
            off = slope * lax.convert_element_type(start, F32)
            chunks = [s[:, j * LANES:(j + 1) * LANES] for j in range(width // LANES)]
            mx = functools.reduce(jnp.maximum, chunks)
            m_prev = m_sc[...]
            m_new = jnp.maximum(m_prev, jnp.max(mx, axis=-1, keepdims=True) + off)
            shift = m_new - off
            ps = [jnp.exp(ch - shift) for ch in chunks]
            alpha = jnp.exp(m_prev - m_new)
            l_sc[...] = alpha * l_sc[...] + functools.reduce(jnp.add, ps)
            p = jnp.concatenate([x.astype(BF16) for x in ps], axis=1)
            acc_sc[...] = alpha * acc_sc[...] + _dot(p, v_ref[0, pl.ds(start, width), :])
            m_sc[...] = m_new

        if qi // 2 > 0:
            def body(i, carry):
                block(pl.multiple_of(i * 2 * blk, 2 * blk), 2 * blk, False)
                return carry
            lax.fori_loop(0, qi // 2, body, 0)
        if qi % 2 == 1:
            block((qi - 1) * blk, blk, False)
        block(qi * blk, blk, True)

        o = acc_sc[...] / jnp.sum(l_sc[...], axis=-1, keepdims=True)
        out = o[:blk] - lam * o[blk:]
        out = _rms(out, sw_ref[...], SUBLN_EPS) * (1.0 - lam_init)
        o_ref[0, qi * blk:(qi + 1) * blk, :] = out.astype(o_ref.dtype)


def attn_prompt(q, k, v, slopes, lams, subln_w, lam_init, name):
    bsz, s, w = q.shape
    heads = slopes.shape[0]
    hd = w // heads
    assert hd == LANES and 8 % heads == 0
    blk = min(256, s)
    wide = min(2 * blk, s)
    assert s % blk == 0 and blk % LANES == 0 and wide <= 2 * ALIBI_SPLIT * ALIBI_SPLIT
    kpos = jnp.arange(wide)
    feat = jnp.zeros((wide, hd), F32).at[:, 0].set((kpos // ALIBI_SPLIT * ALIBI_SPLIT).astype(F32))
    feat = feat.at[:, 1].set((kpos % ALIBI_SPLIT).astype(F32))
    kfeat = (slopes[:, None, None] * feat[None]).astype(BF16)

    lam_specs = [pl.BlockSpec(lams[0].shape, lambda b, h: (0, 0)) for _ in range(4)]
    return pl.pallas_call(
        functools.partial(_attn_prompt_kernel, blk=blk, lam_init=lam_init),
        out_shape=jax.ShapeDtypeStruct((bsz, s, w), BF16),
        grid=(bsz, heads),
        in_specs=[pl.BlockSpec(memory_space=pltpu.SMEM),
                  pl.BlockSpec((1, s, hd), lambda b, h: (b, 0, h)),
                  pl.BlockSpec((1, s, hd), lambda b, h: (b, 0, h)),
                  pl.BlockSpec((1, s, hd), lambda b, h: (b, 0, h)),
                  pl.BlockSpec((1, wide, hd), lambda b, h: (h, 0, 0))] + lam_specs
                 + [pl.BlockSpec((1, hd), lambda b, h: (0, 0))],
        out_specs=pl.BlockSpec((1, s, hd), lambda b, h: (b, 0, h)),
        scratch_shapes=[pltpu.VMEM((2 * blk, LANES), F32), pltpu.VMEM((2 * blk, LANES), F32),
                        pltpu.VMEM((2 * blk, hd), F32)],
        compiler_params=_cparams(("parallel", "parallel")),
        name=name,
    )(slopes, q, k, v, kfeat, *lams, subln_w.reshape(1, hd))


PAGES_PER_STEP = 8


def _attn_sample_kernel(pt_ref, q_ref, *refs, heads, steps, n_pages, pps, lam_init):
    ck_refs, cv_refs = refs[:pps], refs[pps:2 * pps]
    kn_ref, vn_ref, lq1, lk1, lq2, lk2, sw_ref, o_ref, m_sc, l_sc, acc_sc = refs[2 * pps:]
    p = pl.program_id(1)
    hd = sw_ref.shape[-1]
    hrows = 2 * steps
    nrow = heads * hrows
    past = n_pages * PAGE_SIZE
    n_steps = n_pages // pps

    @pl.when(p == 0)
    def _():
        m_sc[...] = jnp.full(m_sc.shape, NEG_BIG, F32)
        l_sc[...] = jnp.zeros(l_sc.shape, F32)
        acc_sc[...] = jnp.zeros(acc_sc.shape, F32)

    ncol = PAGE_SIZE * heads
    r = lax.broadcasted_iota(jnp.int32, (nrow, ncol), 0)
    col = lax.broadcasted_iota(jnp.int32, (nrow, ncol), 1)
    own = (col % heads) == (r // hrows)
    rel = r % steps - col // heads
    r1 = lax.broadcasted_iota(jnp.int32, (nrow, 1), 0)
    slope1 = jnp.exp2(-8.0 * (r1 // hrows + 1).astype(F32) / heads)
    bias0 = -slope1 * rel.astype(F32)

    def update(k_refs, v_refs, keep, page_pos):
        us, cs = [], []
        for j, k_ref in enumerate(k_refs):
            raw = _dot_nt(q_ref[0], k_ref[0].astype(BF16))
            us.append(jnp.where(keep, raw + bias0, NEG_BIG))
            cs.append(-slope1 * lax.convert_element_type(past - page_pos[j], F32))
        m_prev = m_sc[...]
        m_new = m_prev
        for u, c in zip(us, cs):
            m_new = jnp.maximum(m_new, jnp.max(u, axis=-1, keepdims=True) + c)
        corr = jnp.exp(m_prev - m_new)
        l_new = l_sc[...] * corr
        acc = acc_sc[...] * corr
        for u, c, v_ref in zip(us, cs, v_refs):
            pe = jnp.exp(u - (m_new - c))
            l_new = l_new + jnp.sum(pe, axis=-1, keepdims=True)
            acc = acc + _dot(pe.astype(BF16), v_ref[0].astype(BF16))
        l_sc[...] = l_new
        acc_sc[...] = acc
        m_sc[...] = m_new

    @pl.when(p < n_steps)
    def _():
        update(ck_refs, cv_refs, own, [(p * pps + j) * PAGE_SIZE for j in range(pps)])

    @pl.when(p == n_steps)
    def _():
        update([kn_ref], [vn_ref], own & (rel >= 0) & (col // heads < steps), [past])
        od = acc_sc[...] / l_sc[...]
        lam = _lambda_value(lq1, lk1, lq2, lk2, lam_init)
        out = od - lam * pltpu.roll(od, nrow - steps, 0)
        out = _rms(out, sw_ref[...], SUBLN_EPS) * (1.0 - lam_init)
        o_ref[0] = out.astype(o_ref.dtype)


def attn_sample(qh, cache_k, cache_v, knew, vnew, page_table, lams, subln_w, lam_init, steps, name):
    db, heads, hrows, hd = qh.shape
    n_pages = page_table.shape[1]
    pps = math.gcd(PAGES_PER_STEP, n_pages)
    n_steps = n_pages // pps
    pt = page_table.reshape(-1).astype(jnp.int32)
    qh = qh.reshape(db, heads * hrows, hd)
    cache_k, cache_v, knew, vnew = [a.reshape(a.shape[0], PAGE_SIZE * heads, hd) for a in (cache_k, cache_v, knew, vnew)]

    def page_map(j):
        return lambda b, p, pt: (pt[b * n_pages + jnp.minimum(p, n_steps - 1) * pps + j], 0, 0)

    page_specs = [pl.BlockSpec((1, PAGE_SIZE * heads, hd), page_map(j)) for j in range(pps)]
    new_spec = pl.BlockSpec((1, PAGE_SIZE * heads, hd), lambda b, p, pt: (b, 0, 0))
    lam_specs = [pl.BlockSpec(lams[0].shape, lambda b, p, pt: (0, 0)) for _ in range(4)]
    grid_spec = pltpu.PrefetchScalarGridSpec(
        num_scalar_prefetch=1,
        grid=(db, n_steps + 1),
        in_specs=[pl.BlockSpec((1, heads * hrows, hd), lambda b, p, pt: (b, 0, 0))] + page_specs + page_specs
                 + [new_spec, new_spec] + lam_specs + [pl.BlockSpec((1, hd), lambda b, p, pt: (0, 0))],
        out_specs=pl.BlockSpec((1, heads * hrows, hd), lambda b, p, pt: (b, 0, 0)),
        scratch_shapes=[pltpu.VMEM((heads * hrows, 1), F32), pltpu.VMEM((heads * hrows, 1), F32),
                        pltpu.VMEM((heads * hrows, hd), F32)],
    )
    return pl.pallas_call(
        functools.partial(_attn_sample_kernel, heads=heads, steps=steps, n_pages=n_pages, pps=pps,
                          lam_init=lam_init),
        out_shape=jax.ShapeDtypeStruct((db, heads * hrows, hd), F32),
        grid_spec=grid_spec,
        compiler_params=_cparams(("parallel", "arbitrary")),
        name=name,
    )(pt, qh, *([cache_k] * pps), *([cache_v] * pps), knew, vnew, *lams, subln_w.reshape(1, hd))


def _router_weights(w_group, w_sub):
    d = w_group.shape[0]
    ws = jnp.transpose(w_sub, (1, 0, 2)).reshape(d, MOE_GROUPS * MOE_EPG)
    pad = jnp.zeros((d, LANES - MOE_GROUPS - MOE_GROUPS * MOE_EPG), F32)
    return jnp.concatenate([w_group.astype(F32), ws.astype(F32), pad], axis=1)


def _head_queries(q, heads, steps):
    db = q.shape[0]
    dh = q.shape[-1] // heads // 2
    q5 = q.reshape(db, steps, heads, 2, dh)
    eye_c = jnp.eye(2, dtype=q.dtype)
    qh = jnp.einsum('bthcd,ce->bhcted', q5, eye_c)
    return qh.reshape(db, heads, 2 * steps, 2 * dh)


def _trunk(x, s0_all, paged, wts, tag):
    bsz, s, d = x.shape
    t = bsz * s
    depth = wts['norm_mix_w'].shape[0]
    n_a = wts['a_w_in'].shape[0]
    heads_a = s0_all.shape[2]
    heads_b = wts['slopes'].shape[0]
    hd = d // heads_b
    h = x.reshape(t, d)
    states = []
    k_f = v_f = k_b = v_b = None
    precise = paged is not None
    mm = wts['f32'] if precise else wts['bf16']
    for l in range(depth):
        if l < n_a:
            (proj,) = norm_matmul(h, wts['norm_mix_w'][l], mm['a_w_in'][l], [(0, 4 * d, 1.0, 1)], [F32],
                                  f"{tag}_l{l}_inproj")
            proj = proj.reshape(bsz, s, 4 * d)
            if s >= GLA_CHUNK:
                o, s_new = gla(proj, wts['a_lb_logits'], wts['a_gnorm_w'][l], s0_all[l], l, f"{tag}_l{l}_gla")
            else:
                rows = -(-s // SUBLANES) * SUBLANES
                proj = jnp.pad(proj, ((0, 0), (0, rows - s), (0, 0)))
                o, s_new = gla_steps(proj, wts['a_lb_logits'], wts['a_gnorm_w'][l], s0_all[l], l, s,
                                     f"{tag}_l{l}_gla")
                o = o[:, :s]
            states.append(s_new)
            h = matmul_residual(o.reshape(t, d), mm['a_w_o'][l], h, f"{tag}_l{l}_oproj")
        else:
            j = l - n_a
            lam_init = 0.8 - 0.6 * math.exp(-0.3 * l)
            lams = [wts[nm][j].reshape(1, -1) for nm in ('b_lambda_q1', 'b_lambda_k1', 'b_lambda_q2', 'b_lambda_k2')]
            (q,) = norm_matmul(h, wts['norm_mix_w'][l], mm['b_w_q'][j], [(0, d, (hd // 2) ** -0.5, 1)],
                               [F32 if precise else BF16], f"{tag}_l{l}_qproj")
            if paged is None:
                o = attn_prompt(q.reshape(bsz, s, d), k_b.reshape(bsz, s, d), v_b.reshape(bsz, s, d),
                                wts['slopes'], lams, wts['b_subln_w'][j], lam_init, f"{tag}_l{l}_attn")
                o = o.reshape(t, d)
            else:
                cache_k, cache_v, page_table = paged
                qh = _head_queries(q.reshape(bsz, s, d), heads_b, s).astype(BF16)
                padn = ((0, 0), (0, PAGE_SIZE - s), (0, 0), (0, 0))
                o = attn_sample(qh, cache_k, cache_v, jnp.pad(k_f.reshape(bsz, s, heads_b, hd), padn),
                                jnp.pad(v_f.reshape(bsz, s, heads_b, hd), padn), page_table, lams,
                                wts['b_subln_w'][j], lam_init, s, f"{tag}_l{l}_attn")
                o = o.reshape(bsz, heads_b, 2, s, hd)[:, :, 0].transpose(0, 2, 1, 3).reshape(t, d)
            h = matmul_residual(o, mm['b_w_o'][j], h, f"{tag}_l{l}_oproj")
        h = moe_layer(h, wts['norm_ffn_w'][l], wts['router_w_f32' if precise else 'router_w_split'][l],
                      mm['moe_w_gate'][l], mm['moe_w_up'][l], mm['moe_w_down'][l], f"{tag}_l{l}_moe")
        if l == n_a - 1:
            k_f, k_b, v_f, v_b = norm_matmul(h, wts['kv_norm_w'], mm['w_kv'], [(0, d, 1.0, 2), (d, d, 1.0, 2)],
                                            [F32, BF16, F32, BF16], f"{tag}_kvproj")
    y = rmsnorm_rows(h, wts['final_norm_w'], f"{tag}_final_norm")
    return (y.reshape(bsz, s, d), jnp.stack(states),
            k_f.reshape(bsz, s, heads_b, hd), v_f.reshape(bsz, s, heads_b, hd))


def kernel(x_prompt, x_sample, state_hgrn, cache_k, cache_v, page_table, norm_mix_w, norm_ffn_w, a_w_in,
           a_lb_logits, a_gnorm_w, a_w_o, kv_norm_w, w_kv, b_w_q, b_lambda_q1, b_lambda_k1, b_lambda_q2,
           b_lambda_k2, b_subln_w, b_w_o, moe_w_group, moe_w_sub, moe_w_gate, moe_w_up, moe_w_down,
           final_norm_w):
    depth = norm_mix_w.shape[0]
    heads_b = cache_k.shape[2]
    mats = dict(a_w_in=a_w_in, a_w_o=a_w_o, w_kv=w_kv, b_w_q=b_w_q, b_w_o=b_w_o,
                moe_w_gate=moe_w_gate, moe_w_up=moe_w_up, moe_w_down=moe_w_down)
    router_w = jnp.stack([_router_weights(moe_w_group[l], moe_w_sub[l]) for l in range(depth)])
    router_hi = router_w.astype(BF16)
    router_lo = (router_w - router_hi.astype(F32)).astype(BF16)
    wts = dict(
        norm_mix_w=norm_mix_w, norm_ffn_w=norm_ffn_w, a_lb_logits=a_lb_logits, a_w_in=a_w_in,
        a_gnorm_w=a_gnorm_w, kv_norm_w=kv_norm_w, b_lambda_q1=b_lambda_q1, b_lambda_k1=b_lambda_k1,
        b_lambda_q2=b_lambda_q2, b_lambda_k2=b_lambda_k2, b_subln_w=b_subln_w,
        router_w_f32=router_w, router_w_split=jnp.stack([router_hi, router_lo], axis=1),
        final_norm_w=final_norm_w,
        slopes=2.0 ** (-8.0 * jnp.arange(1, heads_b + 1, dtype=F32) / heads_b),
        f32=mats, bf16={k: v.astype(BF16) for k, v in mats.items()},
    )
    n_a = a_w_in.shape[0]
    bsz = x_prompt.shape[0]
    s0_prompt = jnp.zeros((n_a, bsz) + state_hgrn.shape[2:], F32)
    y_p, st_p, k_p, v_p = _trunk(x_prompt, s0_prompt, None, wts, "p")
    paged = (cache_k, cache_v, page_table)
    y_s, st_s, k_s, v_s = _trunk(x_sample, state_hgrn, paged, wts, "s")
    return (y_p, y_s, st_p, st_s, k_p, v_p, k_s, v_s)
```

```python
import functools
import math

import jax
import jax.numpy as jnp
from jax import lax
from jax.experimental import pallas as pl
from jax.experimental.pallas import tpu as pltpu

F32 = jnp.float32
BF16 = jnp.bfloat16
HIGHEST = lax.Precision.HIGHEST

LANES = 128
SUBLANES = 8
VMEM_LIMIT_BYTES = 48 * 1024 * 1024

NORM_EPS = 1e-6
SUBLN_EPS = 1e-5
NEG_BIG = -1e30
GLA_CHUNK = 128
MOE_GROUPS = 4
MOE_EPG = 4
PAGE_SIZE = 128


def _cparams(sem):
    return pltpu.CompilerParams(dimension_semantics=sem, vmem_limit_bytes=VMEM_LIMIT_BYTES)


def _rms(x, w, eps):
    var = jnp.mean(x * x, axis=-1, keepdims=True)
    return x * lax.rsqrt(var + eps) * w


def _sigmoid(x):
    return 1.0 / (1.0 + jnp.exp(-x))


def _dot(a, b):
    return jnp.dot(a, b, preferred_element_type=F32)


def _dot_f32(a, b):
    return jnp.dot(a, b, precision=HIGHEST, preferred_element_type=F32)


def _dot_nt(a, b):
    return lax.dot_general(a, b, (((1,), (1,)), ((), ())), preferred_element_type=F32)


def _dot_tn(a, b):
    return lax.dot_general(a, b, (((0,), (0,)), ((), ())), preferred_element_type=F32)


def _row_tile(t, pref):
    tm = min(pref, t)
    assert t % tm == 0, (t, tm)
    return tm


def _norm_matmul_kernel(x_ref, nw_ref, w_ref, *out_refs, groups, nchunk):
    hn = _rms(x_ref[...], nw_ref[...], NORM_EPS)
    precise = w_ref.dtype == F32
    if not precise:
        hn = hn.astype(BF16)
    k = 0
    for (c0, width, scale, n_out) in groups:
        refs = out_refs[k:k + n_out]
        k += n_out
        for j in range(0, width, nchunk):
            wblk = w_ref[:, c0 + j:c0 + j + nchunk]
            acc = _dot_f32(hn, wblk) if precise else _dot(hn, wblk)
            if scale != 1.0:
                acc = acc * scale
            for r in refs:
                r[:, j:j + nchunk] = acc.astype(r.dtype)


def norm_matmul(x, nw, w, groups, out_dtypes, name):
    t, d = x.shape
    tm = _row_tile(t, 256)
    widths = []
    for (c0, width, scale, n_out) in groups:
        widths += [width] * n_out
    nchunk = min(512, min(widths))
    out_shape = [jax.ShapeDtypeStruct((t, wd), dt) for wd, dt in zip(widths, out_dtypes)]
    return pl.pallas_call(
        functools.partial(_norm_matmul_kernel, groups=tuple(groups), nchunk=nchunk),
        out_shape=out_shape,
        grid=(t // tm,),
        in_specs=[pl.BlockSpec((tm, d), lambda i: (i, 0)),
                  pl.BlockSpec((1, d), lambda i: (0, 0)),
                  pl.BlockSpec(w.shape, lambda i: (0, 0))],
        out_specs=[pl.BlockSpec((tm, wd), lambda i: (i, 0)) for wd in widths],
        compiler_params=_cparams(("parallel",)),
        name=name,
    )(x, nw.reshape(1, d), w)


def _matmul_res_kernel(x_ref, w_ref, r_ref, o_ref):
    if w_ref.dtype == F32:
        o_ref[...] = r_ref[...] + _dot_f32(x_ref[...], w_ref[...])
    else:
        o_ref[...] = r_ref[...] + _dot(x_ref[...], w_ref[...])


def matmul_residual(x, w, res, name):
    t, kdim = x.shape
    n = w.shape[1]
    tm = _row_tile(t, 512)
    return pl.pallas_call(
        _matmul_res_kernel,
        out_shape=jax.ShapeDtypeStruct((t, n), F32),
        grid=(t // tm,),
        in_specs=[pl.BlockSpec((tm, kdim), lambda i: (i, 0)),
                  pl.BlockSpec((kdim, n), lambda i: (0, 0)),
                  pl.BlockSpec((tm, n), lambda i: (i, 0))],
        out_specs=pl.BlockSpec((tm, n), lambda i: (i, 0)),
        compiler_params=_cparams(("parallel",)),
        name=name,
    )(x, w, res)


def _rmsnorm_kernel(x_ref, w_ref, o_ref):
    o_ref[...] = _rms(x_ref[...], w_ref[...], NORM_EPS)


def rmsnorm_rows(x, w, name):
    t, d = x.shape
    tm = _row_tile(t, 512)
    return pl.pallas_call(
        _rmsnorm_kernel,
        out_shape=jax.ShapeDtypeStruct((t, d), F32),
        grid=(t // tm,),
        in_specs=[pl.BlockSpec((tm, d), lambda i: (i, 0)), pl.BlockSpec((1, d), lambda i: (0, 0))],
        out_specs=pl.BlockSpec((tm, d), lambda i: (i, 0)),
        compiler_params=_cparams(("parallel",)),
        name=name,
    )(x, w.reshape(1, d))


def _forget_lower_bound(logits, layer):
    le = jnp.exp(logits - jnp.max(logits, axis=0, keepdims=True))
    lp = le / jnp.sum(le, axis=0, keepdims=True)
    lb = jnp.zeros((1, logits.shape[1]), F32)
    for li in range(1, layer + 1):
        lb = lb + lp[li:li + 1, :]
    return lb


def _gla_kernel(q_ref, f_ref, i_ref, g_ref, lbl_ref, gw_ref, s0_ref, o_ref, sn_ref, st_ref,
                *, chunk, layer, heads, dk):
    n = pl.program_id(1)
    c = chunk

    @pl.when(n == 0)
    def _():
        for h in range(heads):
            st_ref[h] = s0_ref[0, h].T

    lb_all = _forget_lower_bound(lbl_ref[...], layer)

    row = lax.broadcasted_iota(jnp.int32, (c, dk), 0)
    rr = lax.broadcasted_iota(jnp.int32, (c, c), 0)
    cc = lax.broadcasted_iota(jnp.int32, (c, c), 1)
    tri = (rr >= cc).astype(F32)
    ones_b = jnp.ones((dk, c), BF16)
    gw = gw_ref[...]
    nblk = c // SUBLANES
    srow = lax.broadcasted_iota(jnp.int32, (SUBLANES, dk), 0)
    lane_c = lax.broadcasted_iota(jnp.int32, (SUBLANES, c), 1)

    for h in range(heads):
        sl = slice(h * dk, (h + 1) * dk)
        lb = lb_all[:, sl]
        fg = lb + (1.0 - lb) * _sigmoid(f_ref[0, :, sl])
        logf = jnp.log(fg)
        kk = 1.0 - fg
        qf = q_ref[0, :, sl]
        qs = qf * _sigmoid(qf)
        v = i_ref[0, :, sl]
        vb = v.astype(BF16)
        b = lax.dot_general(tri, logf, (((1,), (0,)), ((), ())), precision=HIGHEST,
                            preferred_element_type=F32)
        st = st_ref[h]
        inter = _dot_nt((qs * jnp.exp(b)).astype(BF16), st.astype(BF16))

        att = jnp.zeros((c, c), F32)
        half = c // 2
        while half >= SUBLANES:
            lower = ((row // half) % 2) == 1
            mids = []
            for blk in range(c // (2 * half)):
                mid = b[blk * 2 * half + half - 1: blk * 2 * half + half, :]
                mids.append(jnp.broadcast_to(mid, (2 * half, dk)))
            bmid = mids[0] if len(mids) == 1 else jnp.concatenate(mids, axis=0)
            ex = jnp.exp(jnp.where(lower, b - bmid, bmid - b))
            qd = jnp.where(lower, qs * ex, 0.0).astype(BF16)
            kd = jnp.where(lower, 0.0, kk * ex).astype(BF16)
            a = _dot_nt(qd, kd)
            same = (rr // (2 * half)) == (cc // (2 * half))
            att = att + jnp.where(same, a, 0.0)
            half //= 2

        pieces = []
        for blk in range(nblk):
            r0 = blk * SUBLANES
            bb = b[r0:r0 + SUBLANES, :]
            qb = qs[r0:r0 + SUBLANES, :]
            kb = kk[r0:r0 + SUBLANES, :]
            for s in range(SUBLANES):
                keep = srow >= s
                dec = jnp.exp(jnp.where(keep, bb - bb[s:s + 1, :], 0.0))
                pieces.append(jnp.where(keep, qb * kb[s:s + 1, :] * dec, 0.0).astype(BF16))
        pst = jnp.concatenate(pieces, axis=0)
        red = _dot(pst, ones_b)
        dblocks = []
        for blk in range(nblk):
            acc = jnp.zeros((SUBLANES, c), F32)
            for s in range(SUBLANES):
                idx = blk * SUBLANES + s
                acc = jnp.where(lane_c == idx, red[idx * SUBLANES:(idx + 1) * SUBLANES, :], acc)
            dblocks.append(acc)
        att = att + jnp.concatenate(dblocks, axis=0)

        out = inter + _dot(att.astype(BF16), vb)
        out = _rms(out, gw, NORM_EPS)
        gf = g_ref[0, :, sl]
        o_ref[0, :, sl] = (out * (gf * _sigmoid(gf))).astype(o_ref.dtype)

        bl = b[c - 1:c, :]
        kdl = (kk * jnp.exp(bl - b)).astype(BF16)
        st_new = st * jnp.exp(bl) + _dot_tn(vb, kdl)
        st_ref[h] = st_new

    @pl.when(n == pl.num_programs(1) - 1)
    def _():
        for h in range(heads):
            sn_ref[0, h] = st_ref[h].T


def gla(proj, lb_logits, gnorm_w, s0, layer, name):
    bsz, s, w4 = proj.shape
    w = w4 // 4
    _, heads, dk, dv = s0.shape
    c = min(GLA_CHUNK, s)
    assert s % c == 0 and c % (2 * SUBLANES) == 0 and dk == dv == LANES
    nl = lb_logits.shape[0]
    kern = functools.partial(_gla_kernel, chunk=c, layer=layer, heads=heads, dk=dk)
    return pl.pallas_call(
        kern,
        out_shape=[jax.ShapeDtypeStruct((bsz, s, w), BF16), jax.ShapeDtypeStruct(s0.shape, F32)],
        grid=(bsz, s // c),
        in_specs=[pl.BlockSpec((1, c, w), lambda b, n: (b, n, 0)),
                  pl.BlockSpec((1, c, w), lambda b, n: (b, n, 1)),
                  pl.BlockSpec((1, c, w), lambda b, n: (b, n, 2)),
                  pl.BlockSpec((1, c, w), lambda b, n: (b, n, 3)),
                  pl.BlockSpec((nl, w), lambda b, n: (0, 0)),
                  pl.BlockSpec((1, dv), lambda b, n: (0, 0)),
                  pl.BlockSpec((1, heads, dk, dv), lambda b, n: (b, 0, 0, 0))],
        out_specs=[pl.BlockSpec((1, c, w), lambda b, n: (b, n, 0)),
                   pl.BlockSpec((1, heads, dk, dv), lambda b, n: (b, 0, 0, 0))],
        scratch_shapes=[pltpu.VMEM((heads, dv, dk), F32)],
        compiler_params=_cparams(("parallel", "arbitrary")),
        name=name,
    )(proj, proj, proj, proj, lb_logits, gnorm_w.reshape(1, dv), s0)


def _gla_steps_kernel(p_ref, lbl_ref, gw_ref, s0_ref, o_ref, sn_ref, *, steps, layer, heads, dk):
    w = heads * dk
    rows = p_ref.shape[1]
    lb_all = _forget_lower_bound(lbl_ref[...], layer)
    gw = gw_ref[...]
    for h in range(heads):
        qf = p_ref[0, :, h * dk:(h + 1) * dk]
        ff = p_ref[0, :, w + h * dk:w + (h + 1) * dk]
        v = p_ref[0, :, 2 * w + h * dk:2 * w + (h + 1) * dk]
        gf = p_ref[0, :, 3 * w + h * dk:3 * w + (h + 1) * dk]
        lb = lb_all[:, h * dk:(h + 1) * dk]
        fg = lb + (1.0 - lb) * _sigmoid(ff)
        qs = qf * _sigmoid(qf)
        stack = jnp.concatenate([qs, fg, 1.0 - fg, jnp.zeros((dk - 3 * rows, dk), F32)], axis=0)
        cols = stack.T
        st = s0_ref[0, h]
        outs = []
        for t in range(steps):
            st = st * cols[:, rows + t:rows + t + 1] + cols[:, 2 * rows + t:2 * rows + t + 1] * v[t:t + 1, :]
            outs.append(jnp.sum(st * cols[:, t:t + 1], axis=0, keepdims=True))
        sn_ref[0, h] = st
        out = jnp.concatenate(outs + [jnp.zeros((rows - steps, dk), F32)], axis=0)
        out = _rms(out, gw, NORM_EPS)
        o_ref[0, :, h * dk:(h + 1) * dk] = out * (gf * _sigmoid(gf))


def gla_steps(proj, lb_logits, gnorm_w, s0, layer, steps, name):
    bsz, rows, w4 = proj.shape
    w = w4 // 4
    _, heads, dk, dv = s0.shape
    assert rows % SUBLANES == 0 and 3 * rows <= dk and dk == dv == LANES
    nl = lb_logits.shape[0]
    return pl.pallas_call(
        functools.partial(_gla_steps_kernel, steps=steps, layer=layer, heads=heads, dk=dk),
        out_shape=[jax.ShapeDtypeStruct((bsz, rows, w), F32), jax.ShapeDtypeStruct(s0.shape, F32)],
        grid=(bsz,),
        in_specs=[pl.BlockSpec((1, rows, w4), lambda b: (b, 0, 0)),
                  pl.BlockSpec((nl, w), lambda b: (0, 0)),
                  pl.BlockSpec((1, dv), lambda b: (0, 0)),
                  pl.BlockSpec((1, heads, dk, dv), lambda b: (b, 0, 0, 0))],
        out_specs=[pl.BlockSpec((1, rows, w), lambda b: (b, 0, 0)),
                   pl.BlockSpec((1, heads, dk, dv), lambda b: (b, 0, 0, 0))],
        compiler_params=_cparams(("parallel",)),
        name=name,
    )(proj, lb_logits, gnorm_w.reshape(1, dv), s0)


def _router_logits(hn, wr_ref):
    if wr_ref.dtype == F32:
        return _dot_f32(hn, wr_ref[...])
    hi = hn.astype(BF16)
    lo = (hn - hi.astype(F32)).astype(BF16)
    return _dot(hi, wr_ref[0]) + (_dot(lo, wr_ref[0]) + _dot(hi, wr_ref[1]))


def _router_kernel(h_ref, nw_ref, wr_ref, gid_ref):
    hn = _rms(h_ref[...], nw_ref[...], NORM_EPS)
    logits = _router_logits(hn, wr_ref)
    lane = lax.broadcasted_iota(jnp.int32, logits.shape, 1)
    isg = lane < MOE_GROUPS
    pgl = jnp.where(isg, logits, -jnp.inf)
    m = jnp.max(pgl, axis=-1, keepdims=True)
    gidx = jnp.min(jnp.where(isg & (pgl == m), lane, LANES), axis=-1, keepdims=True)
    rep = jnp.broadcast_to(gidx.astype(F32), logits.shape).astype(BF16)
    avg = jnp.full((SUBLANES, LANES), 1.0 / LANES, BF16)
    gid_ref[...] = _dot_nt(avg, rep).astype(jnp.int32)


def moe_router(h, nw, wr, name):
    t, d = h.shape
    tm = _row_tile(t, 512)
    gid = pl.pallas_call(
        _router_kernel,
        out_shape=jax.ShapeDtypeStruct((t // tm * SUBLANES, tm), jnp.int32),
        grid=(t // tm,),
        in_specs=[pl.BlockSpec((tm, d), lambda i: (i, 0)),
                  pl.BlockSpec((1, d), lambda i: (0, 0)),
                  pl.BlockSpec(wr.shape, lambda i: (0,) * wr.ndim)],
        out_specs=pl.BlockSpec((SUBLANES, tm), lambda i: (i, 0)),
        compiler_params=_cparams(("parallel",)),
        name=name,
    )(h, nw.reshape(1, d), wr)
    return gid.reshape(t // tm, SUBLANES, tm)[:, 0, :].reshape(t)


DMA_UNROLL = 8


def _moe_ffn_kernel(tg_ref, nv_ref, ridx_ref, h_hbm, nw_ref, wr_ref, wg_ref, wu_ref, wd_ref, out_hbm,
                    xbuf, obuf, gsem, ssem, *, tm):
    t = pl.program_id(0)
    n_tiles = pl.num_programs(0)
    nv = nv_ref[t]
    g = tg_ref[t]
    slot = t % 2

    def gather_rows(tile, sl):
        def body(i, carry):
            for u in range(DMA_UNROLL):
                r = i * DMA_UNROLL + u
                pltpu.make_async_copy(h_hbm.at[pl.ds(ridx_ref[tile * tm + r], 1)],
                                      xbuf.at[sl, pl.ds(r, 1)], gsem.at[sl]).start()
            return carry
        lax.fori_loop(0, tm // DMA_UNROLL, body, 0)

    def scatter_row(r):
        pltpu.make_async_copy(obuf.at[pl.ds(r, 1)], out_hbm.at[pl.ds(ridx_ref[t * tm + r], 1)], ssem).start()

    def wait_scatter(count):
        groups = count // SUBLANES

        @pl.when(groups > 0)
        def _():
            n = pl.multiple_of(groups * SUBLANES, SUBLANES)
            pltpu.make_async_copy(obuf.at[pl.ds(0, n)], out_hbm.at[pl.ds(0, n)], ssem).wait()

        def one(r, carry):
            pltpu.make_async_copy(obuf.at[pl.ds(0, 1)], out_hbm.at[pl.ds(0, 1)], ssem).wait()
            return carry
        lax.fori_loop(groups * SUBLANES, count, one, 0)

    def wait_gather(sl):
        pltpu.make_async_copy(h_hbm.at[pl.ds(0, tm)], xbuf.at[sl], gsem.at[sl]).wait()

    @pl.when(t == 0)
    def _():
        gather_rows(t, slot)

    nxt = jnp.minimum(t + 1, n_tiles - 1)
    has_next = (t + 1 < n_tiles) & (nv_ref[nxt] > 0)

    @pl.when(nv > 0)
    def _():
        wait_gather(slot)
        x = xbuf[slot]
        for r in range(tm):
            pltpu.make_async_copy(h_hbm.at[pl.ds(ridx_ref[nxt * tm + r], 1)],
                                  xbuf.at[1 - slot, pl.ds(r, 1)], gsem.at[1 - slot]).start()
        hn = _rms(x, nw_ref[...], NORM_EPS)
        logits = _router_logits(hn, wr_ref)
        lane = lax.broadcasted_iota(jnp.int32, logits.shape, 1)
        isg = lane < MOE_GROUPS
        m = jnp.max(jnp.where(isg, logits, -jnp.inf), axis=-1, keepdims=True)
        e = jnp.where(isg, jnp.exp(jnp.where(isg, logits - m, 0.0)), 0.0)
        gprob = jnp.sum(jnp.where(lane == g, e, 0.0), axis=-1, keepdims=True) / jnp.sum(e, axis=-1, keepdims=True)
        lo = MOE_GROUPS + MOE_EPG * g
        inb = (lane >= lo) & (lane < lo + MOE_EPG)
        sv = jnp.where(inb, logits, -jnp.inf)
        m1 = jnp.max(sv, axis=-1, keepdims=True)
        i1 = jnp.min(jnp.where(inb & (sv == m1), lane, LANES), axis=-1, keepdims=True)
        inb2 = inb & (lane != i1)
        sv2 = jnp.where(inb2, logits, -jnp.inf)
        m2 = jnp.max(sv2, axis=-1, keepdims=True)
        i2 = jnp.min(jnp.where(inb2 & (sv2 == m2), lane, LANES), axis=-1, keepdims=True)
        e2 = jnp.exp(m2 - m1)
        w1 = gprob / (1.0 + e2)
        w2 = gprob * e2 / (1.0 + e2)
        gate = jnp.where(lane == i1, w1, jnp.where(lane == i2, w2, 0.0))

        precise = wg_ref.dtype == F32
        hb = hn if precise else hn.astype(BF16)
        dot = _dot_f32 if precise else _dot
        y = jnp.zeros(x.shape, F32)
        for ex in range(MOE_EPG):
            ge = jnp.sum(jnp.where(lane == lo + ex, gate, 0.0), axis=-1, keepdims=True)
            a = dot(hb, wg_ref[ex])
            u = dot(hb, wu_ref[ex])
            hid = a * _sigmoid(a) * u * ge
            y = y + dot(hid if precise else hid.astype(BF16), wd_ref[ex])
        @pl.when(t > 0)
        def _():
            wait_scatter(nv_ref[jnp.maximum(t - 1, 0)])

        obuf[...] = x + y

        nfull = nv // DMA_UNROLL

        def issue_out(i, carry):
            for u in range(DMA_UNROLL):
                scatter_row(i * DMA_UNROLL + u)
            return carry

        def issue_out_tail(r, carry):
            scatter_row(r)
            return carry

        @pl.when(nv == tm)
        def _():
            for r in range(tm):
                scatter_row(r)

        @pl.when(nv < tm)
        def _():
            lax.fori_loop(0, nfull, issue_out, 0)
            lax.fori_loop(nfull * DMA_UNROLL, nv, issue_out_tail, 0)

        @pl.when(jnp.logical_not(has_next))
        def _():
            wait_scatter(nv)
            wait_gather(1 - slot)


def _moe_plan(gidx, t, tm):
    ng = MOE_GROUPS
    n_tiles = t // tm + ng
    counts = jnp.sum((gidx[:, None] == jnp.arange(ng)[None, :]).astype(jnp.int32), axis=0)
    tiles_g = (counts + tm - 1) // tm
    tiles_end = jnp.cumsum(tiles_g)
    first_tile = tiles_end - tiles_g
    start_g = jnp.cumsum(counts) - counts
    used = tiles_end[-1]
    perm = jnp.argsort(gidx, stable=True).astype(jnp.int32)
    tile_ids = jnp.arange(n_tiles, dtype=jnp.int32)
    tg = jnp.sum((tile_ids[:, None] >= tiles_end[None, :]).astype(jnp.int32), axis=1)
    last_g = jnp.sum((used - 1 >= tiles_end).astype(jnp.int32))
    tg = jnp.where(tile_ids < used, jnp.minimum(tg, ng - 1), last_g).astype(jnp.int32)
    j = tile_ids - first_tile[tg]
    nv = jnp.clip(counts[tg] - j * tm, 0, tm)
    nv = jnp.where(tile_ids < used, nv, 0).astype(jnp.int32)
    src0 = start_g[tg] + j * tm
    src = src0[:, None] + jnp.arange(tm, dtype=jnp.int32)[None, :]
    ok = jnp.arange(tm, dtype=jnp.int32)[None, :] < nv[:, None]
    src = jnp.where(ok, src, src0[:, None])
    ridx = perm[jnp.clip(src, 0, t - 1)].reshape(-1).astype(jnp.int32)
    return tg, nv, ridx


def moe_layer(h, nw, wr, wg, wu, wd, name):
    t, d = h.shape
    tm = _row_tile(t, 256)
    assert tm % DMA_UNROLL == 0
    gid = moe_router(h, nw, wr, name + "_router")
    tg, nv, ridx = _moe_plan(gid, t, tm)
    n_tiles = t // tm + MOE_GROUPS
    fdim = wg.shape[2]
    grid_spec = pltpu.PrefetchScalarGridSpec(
        num_scalar_prefetch=3,
        grid=(n_tiles,),
        in_specs=[pl.BlockSpec(memory_space=pl.ANY),
                  pl.BlockSpec((1, d), lambda i, tg, nv, ri: (0, 0)),
                  pl.BlockSpec(wr.shape, lambda i, tg, nv, ri: (0,) * wr.ndim),
                  pl.BlockSpec((MOE_EPG, d, fdim), lambda i, tg, nv, ri: (tg[i], 0, 0)),
                  pl.BlockSpec((MOE_EPG, d, fdim), lambda i, tg, nv, ri: (tg[i], 0, 0)),
                  pl.BlockSpec((MOE_EPG, fdim, d), lambda i, tg, nv, ri: (tg[i], 0, 0))],
        out_specs=pl.BlockSpec(memory_space=pl.ANY),
        scratch_shapes=[pltpu.VMEM((2, tm, d), F32), pltpu.VMEM((tm, d), F32),
                        pltpu.SemaphoreType.DMA((2,)), pltpu.SemaphoreType.DMA],
    )
    return pl.pallas_call(
        functools.partial(_moe_ffn_kernel, tm=tm),
        out_shape=jax.ShapeDtypeStruct((t, d), F32),
        grid_spec=grid_spec,
        compiler_params=_cparams(("arbitrary",)),
        name=name + "_ffn",
    )(tg, nv, ridx, h, nw.reshape(1, d), wr, wg, wu, wd)


def _lambda_value(lq1, lk1, lq2, lk2, lam_init):
    s1 = jnp.sum(lq1[...] * lk1[...], axis=-1, keepdims=True)
    s2 = jnp.sum(lq2[...] * lk2[...], axis=-1, keepdims=True)
    return jnp.exp(s1) - jnp.exp(s2) + lam_init


ALIBI_SPLIT = 16


def _attn_prompt_kernel(slopes_ref, q_ref, k_ref, v_ref, kf_ref, lq1, lk1, lq2, lk2, sw_ref, o_ref,
                        m_sc, l_sc, acc_sc, *, blk, lam_init):
    h = pl.program_id(1)
    s_len, hd = q_ref.shape[1], q_ref.shape[2]
    nq = s_len // blk
    rows = 2 * blk
    slope = slopes_ref[h]
    lam = _lambda_value(lq1, lk1, lq2, lk2, lam_init)
    lane = lax.broadcasted_iota(jnp.int32, (blk, hd), 1)
    r = lax.broadcasted_iota(jnp.int32, (rows, blk), 0)
    c = lax.broadcasted_iota(jnp.int32, (rows, blk), 1)
    causal = jnp.where(r >= blk, r - blk, r) >= c
    qfeat = jnp.where(lax.broadcasted_iota(jnp.int32, (rows, hd), 1) < 2, 1.0, 0.0).astype(BF16)
    kfeat = kf_ref[0]

    for qi in range(nq):
        q = q_ref[0, qi * blk:(qi + 1) * blk, :]
        zero = jnp.zeros_like(q)
        qa = jnp.concatenate([jnp.concatenate([jnp.where(lane < hd // 2, q, zero),
                                               jnp.where(lane >= hd // 2, q, zero)], axis=0), qfeat], axis=1)
        m_sc[...] = jnp.full(m_sc.shape, NEG_BIG, F32)
        l_sc[...] = jnp.zeros(l_sc.shape, F32)
        acc_sc[...] = jnp.zeros(acc_sc.shape, F32)

        def block(start, width, masked, qa=qa):
            ka = jnp.concatenate([k_ref[0, pl.ds(start, width), :], kfeat[:width]], axis=1)
            s = _dot_nt(qa, ka)
            if masked:
                s = jnp.where(causal, s, NEG_BIG)
            off = slope * lax.convert_element_type(start, F32)
            chunks = [s[:, j * LANES:(j + 1) * LANES] for j in range(width // LANES)]
            mx = functools.reduce(jnp.maximum, chunks)
            m_prev = m_sc[...]
            m_new = jnp.maximum(m_prev, jnp.max(mx, axis=-1, keepdims=True) + off)
            shift = m_new - off
            ps = [jnp.exp(ch - shift) for ch in chunks]
            alpha = jnp.exp(m_prev - m_new)
            l_sc[...] = alpha * l_sc[...] + functools.reduce(jnp.add, ps)
            p = jnp.concatenate([x.astype(BF16) for x in ps], axis=1)
            acc_sc[...] = alpha * acc_sc[...] + _dot(p, v_ref[0, pl.ds(start, width), :])
            m_sc[...] = m_new

        if qi // 2 > 0:
            def body(i, carry):
                block(pl.multiple_of(i * 2 * blk, 2 * blk), 2 * blk, False)
                return carry
            lax.fori_loop(0, qi // 2, body, 0)
        if qi % 2 == 1:
            block((qi - 1) * blk, blk, False)
        block(qi * blk, blk, True)

        o = acc_sc[...] / jnp.sum(l_sc[...], axis=-1, keepdims=True)
        out = o[:blk] - lam * o[blk:]
        out = _rms(out, sw_ref[...], SUBLN_EPS) * (1.0 - lam_init)
        o_ref[0, qi * blk:(qi + 1) * blk, :] = out.astype(o_ref.dtype)


def attn_prompt(q, k, v, slopes, lams, subln_w, lam_init, name):
    bsz, s, w = q.shape
    heads = slopes.shape[0]
    hd = w // heads
    assert hd == LANES and 8 % heads == 0
    blk = min(256, s)
    wide = min(2 * blk, s)
    assert s % blk == 0 and blk % LANES == 0 and wide <= 2 * ALIBI_SPLIT * ALIBI_SPLIT
    kpos = jnp.arange(wide)
    feat = jnp.zeros((wide, hd), F32).at[:, 0].set((kpos // ALIBI_SPLIT * ALIBI_SPLIT).astype(F32))
    feat = feat.at[:, 1].set((kpos % ALIBI_SPLIT).astype(F32))
    kfeat = (slopes[:, None, None] * feat[None]).astype(BF16)

    lam_specs = [pl.BlockSpec(lams[0].shape, lambda b, h: (0, 0)) for _ in range(4)]
    return pl.pallas_call(
        functools.partial(_attn_prompt_kernel, blk=blk, lam_init=lam_init),
        out_shape=jax.ShapeDtypeStruct((bsz, s, w), BF16),
        grid=(bsz, heads),
        in_specs=[pl.BlockSpec(memory_space=pltpu.SMEM),
                  pl.BlockSpec((1, s, hd), lambda b, h: (b, 0, h)),
                  pl.BlockSpec((1, s, hd), lambda b, h: (b, 0, h)),
                  pl.BlockSpec((1, s, hd), lambda b, h: (b, 0, h)),
                  pl.BlockSpec((1, wide, hd), lambda b, h: (h, 0, 0))] + lam_specs
                 + [pl.BlockSpec((1, hd), lambda b, h: (0, 0))],
        out_specs=pl.BlockSpec((1, s, hd), lambda b, h: (b, 0, h)),
        scratch_shapes=[pltpu.VMEM((2 * blk, LANES), F32), pltpu.VMEM((2 * blk, LANES), F32),
                        pltpu.VMEM((2 * blk, hd), F32)],
        compiler_params=_cparams(("parallel", "parallel")),
        name=name,
    )(slopes, q, k, v, kfeat, *lams, subln_w.reshape(1, hd))


PAGES_PER_STEP = 16


def _attn_sample_kernel(pt_ref, q_ref, *refs, heads, steps, n_pages, pps, lam_init):
    ck_refs, cv_refs = refs[:pps], refs[pps:2 * pps]
    kn_ref, vn_ref, lq1, lk1, lq2, lk2, sw_ref, o_ref, m_sc, l_sc, acc_sc = refs[2 * pps:]
    p = pl.program_id(1)
    hd = sw_ref.shape[-1]
    hrows = 2 * steps
    nrow = heads * hrows
    past = n_pages * PAGE_SIZE
    n_steps = n_pages // pps

    @pl.when(p == 0)
    def _():
        m_sc[...] = jnp.full(m_sc.shape, NEG_BIG, F32)
        l_sc[...] = jnp.zeros(l_sc.shape, F32)
        acc_sc[...] = jnp.zeros(acc_sc.shape, F32)

    ncol = PAGE_SIZE * heads
    r = lax.broadcasted_iota(jnp.int32, (nrow, ncol), 0)
    col = lax.broadcasted_iota(jnp.int32, (nrow, ncol), 1)
    own = (col % heads) == (r // hrows)
    rel = r % steps - col // heads
    r1 = lax.broadcasted_iota(jnp.int32, (nrow, 1), 0)
    slope1 = jnp.exp2(-8.0 * (r1 // hrows + 1).astype(F32) / heads)
    bias0 = -slope1 * rel.astype(F32)

    def update(k_refs, v_refs, keep, page_pos):
        us, cs = [], []
        for j, k_ref in enumerate(k_refs):
            raw = _dot_nt(q_ref[0], k_ref[0].astype(BF16))
            us.append(jnp.where(keep, raw + bias0, NEG_BIG))
            cs.append(-slope1 * lax.convert_element_type(past - page_pos[j], F32))
        m_prev = m_sc[...]
        m_new = m_prev
        for u, c in zip(us, cs):
            m_new = jnp.maximum(m_new, jnp.max(u, axis=-1, keepdims=True) + c)
        corr = jnp.exp(m_prev - m_new)
        l_new = l_sc[...] * corr
        acc = acc_sc[...] * corr
        for u, c, v_ref in zip(us, cs, v_refs):
            pe = jnp.exp(u - (m_new - c))
            l_new = l_new + jnp.sum(pe, axis=-1, keepdims=True)
            acc = acc + _dot(pe.astype(BF16), v_ref[0].astype(BF16))
        l_sc[...] = l_new
        acc_sc[...] = acc
        m_sc[...] = m_new

    @pl.when(p < n_steps)
    def _():
        update(ck_refs, cv_refs, own, [(p * pps + j) * PAGE_SIZE for j in range(pps)])

    @pl.when(p == n_steps)
    def _():
        update([kn_ref], [vn_ref], own & (rel >= 0) & (col // heads < steps), [past])
        od = acc_sc[...] / l_sc[...]
        lam = _lambda_value(lq1, lk1, lq2, lk2, lam_init)
        out = od - lam * pltpu.roll(od, nrow - steps, 0)
        out = _rms(out, sw_ref[...], SUBLN_EPS) * (1.0 - lam_init)
        o_ref[0] = out.astype(o_ref.dtype)


def attn_sample(qh, cache_k, cache_v, knew, vnew, page_table, lams, subln_w, lam_init, steps, name):
    db, heads, hrows, hd = qh.shape
    n_pages = page_table.shape[1]
    pps = math.gcd(PAGES_PER_STEP, n_pages)
    n_steps = n_pages // pps
    pt = page_table.reshape(-1).astype(jnp.int32)
    qh = qh.reshape(db, heads * hrows, hd)
    cache_k, cache_v, knew, vnew = [a.reshape(a.shape[0], PAGE_SIZE * heads, hd) for a in (cache_k, cache_v, knew, vnew)]

    def page_map(j):
        return lambda b, p, pt: (pt[b * n_pages + jnp.minimum(p, n_steps - 1) * pps + j], 0, 0)

    page_specs = [pl.BlockSpec((1, PAGE_SIZE * heads, hd), page_map(j)) for j in range(pps)]
    new_spec = pl.BlockSpec((1, PAGE_SIZE * heads, hd), lambda b, p, pt: (b, 0, 0))
    lam_specs = [pl.BlockSpec(lams[0].shape, lambda b, p, pt: (0, 0)) for _ in range(4)]
    grid_spec = pltpu.PrefetchScalarGridSpec(
        num_scalar_prefetch=1,
        grid=(db, n_steps + 1),
        in_specs=[pl.BlockSpec((1, heads * hrows, hd), lambda b, p, pt: (b, 0, 0))] + page_specs + page_specs
                 + [new_spec, new_spec] + lam_specs + [pl.BlockSpec((1, hd), lambda b, p, pt: (0, 0))],
        out_specs=pl.BlockSpec((1, heads * hrows, hd), lambda b, p, pt: (b, 0, 0)),
        scratch_shapes=[pltpu.VMEM((heads * hrows, 1), F32), pltpu.VMEM((heads * hrows, 1), F32),
                        pltpu.VMEM((heads * hrows, hd), F32)],
    )
    return pl.pallas_call(
        functools.partial(_attn_sample_kernel, heads=heads, steps=steps, n_pages=n_pages, pps=pps,
                          lam_init=lam_init),
        out_shape=jax.ShapeDtypeStruct((db, heads * hrows, hd), F32),
        grid_spec=grid_spec,
        compiler_params=_cparams(("parallel", "arbitrary")),
        name=name,
    )(pt, qh, *([cache_k] * pps), *([cache_v] * pps), knew, vnew, *lams, subln_w.reshape(1, hd))


def _router_weights(w_group, w_sub):
    d = w_group.shape[0]
    ws = jnp.transpose(w_sub, (1, 0, 2)).reshape(d, MOE_GROUPS * MOE_EPG)
    pad = jnp.zeros((d, LANES - MOE_GROUPS - MOE_GROUPS * MOE_EPG), F32)
    return jnp.concatenate([w_group.astype(F32), ws.astype(F32), pad], axis=1)


def _head_queries(q, heads, steps):
    db = q.shape[0]
    dh = q.shape[-1] // heads // 2
    q5 = q.reshape(db, steps, heads, 2, dh)
    eye_c = jnp.eye(2, dtype=q.dtype)
    qh = jnp.einsum('bthcd,ce->bhcted', q5, eye_c)
    return qh.reshape(db, heads, 2 * steps, 2 * dh)


def _trunk(x, s0_all, paged, wts, tag):
    bsz, s, d = x.shape
    t = bsz * s
    depth = wts['norm_mix_w'].shape[0]
    n_a = wts['a_w_in'].shape[0]
    heads_a = s0_all.shape[2]
    heads_b = wts['slopes'].shape[0]
    hd = d // heads_b
    h = x.reshape(t, d)
    states = []
    k_f = v_f = k_b = v_b = None
    precise = paged is not None
    mm = wts['f32'] if precise else wts['bf16']
    for l in range(depth):
        if l < n_a:
            (proj,) = norm_matmul(h, wts['norm_mix_w'][l], mm['a_w_in'][l], [(0, 4 * d, 1.0, 1)], [F32],
                                  f"{tag}_l{l}_inproj")
            proj = proj.reshape(bsz, s, 4 * d)
            if s >= GLA_CHUNK:
                o, s_new = gla(proj, wts['a_lb_logits'], wts['a_gnorm_w'][l], s0_all[l], l, f"{tag}_l{l}_gla")
            else:
                rows = -(-s // SUBLANES) * SUBLANES
                proj = jnp.pad(proj, ((0, 0), (0, rows - s), (0, 0)))
                o, s_new = gla_steps(proj, wts['a_lb_logits'], wts['a_gnorm_w'][l], s0_all[l], l, s,
                                     f"{tag}_l{l}_gla")
                o = o[:, :s]
            states.append(s_new)
            h = matmul_residual(o.reshape(t, d), mm['a_w_o'][l], h, f"{tag}_l{l}_oproj")
        else:
            j = l - n_a
            lam_init = 0.8 - 0.6 * math.exp(-0.3 * l)
            lams = [wts[nm][j].reshape(1, -1) for nm in ('b_lambda_q1', 'b_lambda_k1', 'b_lambda_q2', 'b_lambda_k2')]
            (q,) = norm_matmul(h, wts['norm_mix_w'][l], mm['b_w_q'][j], [(0, d, (hd // 2) ** -0.5, 1)],
                               [F32 if precise else BF16], f"{tag}_l{l}_qproj")
            if paged is None:
                o = attn_prompt(q.reshape(bsz, s, d), k_b.reshape(bsz, s, d), v_b.reshape(bsz, s, d),
                                wts['slopes'], lams, wts['b_subln_w'][j], lam_init, f"{tag}_l{l}_attn")
                o = o.reshape(t, d)
            else:
                cache_k, cache_v, page_table = paged
                qh = _head_queries(q.reshape(bsz, s, d), heads_b, s).astype(BF16)
                padn = ((0, 0), (0, PAGE_SIZE - s), (0, 0), (0, 0))
                o = attn_sample(qh, cache_k, cache_v, jnp.pad(k_f.reshape(bsz, s, heads_b, hd), padn),
                                jnp.pad(v_f.reshape(bsz, s, heads_b, hd), padn), page_table, lams,
                                wts['b_subln_w'][j], lam_init, s, f"{tag}_l{l}_attn")
                o = o.reshape(bsz, heads_b, 2, s, hd)[:, :, 0].transpose(0, 2, 1, 3).reshape(t, d)
            h = matmul_residual(o, mm['b_w_o'][j], h, f"{tag}_l{l}_oproj")
        h = moe_layer(h, wts['norm_ffn_w'][l], wts['router_w_f32' if precise else 'router_w_split'][l],
                      mm['moe_w_gate'][l], mm['moe_w_up'][l], mm['moe_w_down'][l], f"{tag}_l{l}_moe")
        if l == n_a - 1:
            k_f, k_b, v_f, v_b = norm_matmul(h, wts['kv_norm_w'], mm['w_kv'], [(0, d, 1.0, 2), (d, d, 1.0, 2)],
                                            [F32, BF16, F32, BF16], f"{tag}_kvproj")
    y = rmsnorm_rows(h, wts['final_norm_w'], f"{tag}_final_norm")
    return (y.reshape(bsz, s, d), jnp.stack(states),
            k_f.reshape(bsz, s, heads_b, hd), v_f.reshape(bsz, s, heads_b, hd))


def kernel(x_prompt, x_sample, state_hgrn, cache_k, cache_v, page_table, norm_mix_w, norm_ffn_w, a_w_in,
           a_lb_logits, a_gnorm_w, a_w_o, kv_norm_w, w_kv, b_w_q, b_lambda_q1, b_lambda_k1, b_lambda_q2,
           b_lambda_k2, b_subln_w, b_w_o, moe_w_group, moe_w_sub, moe_w_gate, moe_w_up, moe_w_down,
           final_norm_w):
    depth = norm_mix_w.shape[0]
    heads_b = cache_k.shape[2]
    mats = dict(a_w_in=a_w_in, a_w_o=a_w_o, w_kv=w_kv, b_w_q=b_w_q, b_w_o=b_w_o,
                moe_w_gate=moe_w_gate, moe_w_up=moe_w_up, moe_w_down=moe_w_down)
    router_w = jnp.stack([_router_weights(moe_w_group[l], moe_w_sub[l]) for l in range(depth)])
    router_hi = router_w.astype(BF16)
    router_lo = (router_w - router_hi.astype(F32)).astype(BF16)
    wts = dict(
        norm_mix_w=norm_mix_w, norm_ffn_w=norm_ffn_w, a_lb_logits=a_lb_logits, a_w_in=a_w_in,
        a_gnorm_w=a_gnorm_w, kv_norm_w=kv_norm_w, b_lambda_q1=b_lambda_q1, b_lambda_k1=b_lambda_k1,
        b_lambda_q2=b_lambda_q2, b_lambda_k2=b_lambda_k2, b_subln_w=b_subln_w,
        router_w_f32=router_w, router_w_split=jnp.stack([router_hi, router_lo], axis=1),
        final_norm_w=final_norm_w,
        slopes=2.0 ** (-8.0 * jnp.arange(1, heads_b + 1, dtype=F32) / heads_b),
        f32=mats, bf16={k: v.astype(BF16) for k, v in mats.items()},
    )
    n_a = a_w_in.shape[0]
    bsz = x_prompt.shape[0]
    s0_prompt = jnp.zeros((n_a, bsz) + state_hgrn.shape[2:], F32)
    y_p, st_p, k_p, v_p = _trunk(x_prompt, s0_prompt, None, wts, "p")
    paged = (cache_k, cache_v, page_table)
    y_s, st_s, k_s, v_s = _trunk(x_sample, state_hgrn, paged, wts, "s")
    return (y_p, y_s, st_p, st_s, k_p, v_p, k_s, v_s)
```
